```python
import jax, jax.numpy as jnp
from jax import lax
import numpy as np

D_MODEL = 1024
BATCH = 2
SEQ = 8192
DEPTH = 4
DEC_BATCH = 128
DEC_SEQ = 1
PAST_LEN = 8192
PAGE_SIZE = 128

MIX_WIDTH = D_MODEL
POOL_CH = D_MODEL // 4
POOL_WINDOWS = (2, 4, 8, 16)
POOL_GROUPS = len(POOL_WINDOWS)
POOL_GROUP_DIM = POOL_CH // POOL_GROUPS
POOL_STATE = max(POOL_WINDOWS) - 1
MLA_HEADS = 8
MLA_V_DIM = 64
MLA_CH = MLA_HEADS * MLA_V_DIM
MLA_NOPE = 64
MLA_ROPE = 32
MLA_KV_RANK = 256
MLA_Q_RANK = 384
ROPE_THETA = 10000.0
SM_SCALE = (MLA_NOPE + MLA_ROPE) ** -0.5
CONV_CH = MIX_WIDTH - POOL_CH - MLA_CH
DW_CONV_LEN = 31
CONV_STATE = DW_CONV_LEN - 1
IN_WIDTH = POOL_CH + MLA_Q_RANK + MLA_KV_RANK + MLA_ROPE + 2 * CONV_CH
SPLIT_POINTS = (POOL_CH, POOL_CH + MLA_Q_RANK, POOL_CH + MLA_Q_RANK + MLA_KV_RANK,
                POOL_CH + MLA_Q_RANK + MLA_KV_RANK + MLA_ROPE)
D_FF = (((8 * D_MODEL + 2) // 3 + 255) // 256) * 256
Q_BLOCK = 128
NORM_EPS = 1e-6
LN_EPS = 1e-5

kernel_name = 'hymba_pool_mla_conformer_decoder_step'


def rmsnorm(x, g):
    xf = x.astype(jnp.float32)
    y = xf * lax.rsqrt(jnp.mean(xf * xf, axis=-1, keepdims=True) + NORM_EPS)
    return (y * g.astype(jnp.float32)).astype(x.dtype)


def layernorm(x, g, b):
    xf = x.astype(jnp.float32)
    mu = jnp.mean(xf, axis=-1, keepdims=True)
    var = jnp.mean(jnp.square(xf - mu), axis=-1, keepdims=True)
    y = (xf - mu) * lax.rsqrt(var + LN_EPS) * g.astype(jnp.float32) + b.astype(jnp.float32)
    return y.astype(x.dtype)


def rope_angles(pos):
    inv = ROPE_THETA ** (-jnp.arange(0, MLA_ROPE, 2, dtype=jnp.float32) / MLA_ROPE)
    ang = pos.astype(jnp.float32)[:, None] * inv[None, :]
    return jnp.cos(ang), jnp.sin(ang)


def apply_rope(x, cos, sin):
    xf = x.astype(jnp.float32)
    x1, x2 = jnp.split(xf, 2, axis=-1)
    return jnp.concatenate([x1 * cos - x2 * sin, x1 * sin + x2 * cos], axis=-1).astype(x.dtype)


def pool_mixer(u, prefix, start, w_pool, scale):
    B, T, C = u.shape
    L = POOL_STATE
    ext = jnp.concatenate([prefix, u], axis=1)
    cs = jnp.cumsum(ext.astype(jnp.float32), axis=1)
    cs = jnp.concatenate([jnp.zeros_like(cs[:, :1]), cs], axis=1)
    end = cs[:, L + 1:]
    pos = start + jnp.arange(T)
    means = []
    for g, w in enumerate(POOL_WINDOWS):
        sl = slice(g * POOL_GROUP_DIM, (g + 1) * POOL_GROUP_DIM)
        wsum = end[..., sl] - cs[:, L + 1 - w:L + 1 - w + T, sl]
        cnt = jnp.minimum(pos + 1, w).astype(jnp.float32)[None, :, None]
        means.append(wsum / cnt)
    d = jnp.concatenate(means, axis=-1) - u.astype(jnp.float32)
    d = d.reshape(B, T, POOL_GROUPS, POOL_GROUP_DIM)
    y = jnp.einsum('btgc,gcd->btgd', d, w_pool.astype(jnp.float32)).reshape(B, T, C)
    y = y * scale.astype(jnp.float32)
    return y.astype(u.dtype), ext[:, -L:]


def conv_mixer(a, prefix, w_dw, b_dw, g_ln, b_ln, w_pw):
    val, gate = jnp.split(a, 2, axis=-1)
    g = val * jax.nn.sigmoid(gate)
    ext = jnp.concatenate([prefix, g], axis=1)
    y = lax.conv_general_dilated(ext, w_dw[:, None, :], (1,), 'VALID',
                                 dimension_numbers=('NWC', 'WIO', 'NWC'),
                                 feature_group_count=CONV_CH) + b_dw
    y = jax.nn.silu(layernorm(y, g_ln, b_ln))
    return y @ w_pw, ext[:, -CONV_STATE:]


def mla_project(c_q, c_kv, k_pe, cos, sin, g_q, w_uq, g_kv, w_uk):
    c_q = rmsnorm(c_q, g_q)
    q = jnp.einsum('bsr,rhd->bshd', c_q, w_uq)
    q_nope, q_pe = q[..., :MLA_NOPE], q[..., MLA_NOPE:]
    q_pe = apply_rope(q_pe, cos[:, None, :], sin[:, None, :])
    q_lat = jnp.einsum('bshn,hrn->bshr', q_nope, w_uk)
    ckv = rmsnorm(c_kv, g_kv)
    kpe = apply_rope(k_pe, cos, sin)
    return q_lat, q_pe, ckv, kpe


def mla_attend_prompt(q_lat, q_pe, ckv, kpe):
    B, S, H, R = q_lat.shape
    nb = S // Q_BLOCK
    ql = q_lat.reshape(B, nb, Q_BLOCK, H, R).transpose(1, 0, 2, 3, 4)
    qp = q_pe.reshape(B, nb, Q_BLOCK, H, MLA_ROPE).transpose(1, 0, 2, 3, 4)
    kpos = jnp.arange(S)

    def block(args):
        i, qlb, qpb = args
        s = (jnp.einsum('bqhr,bkr->bhqk', qlb, ckv, preferred_element_type=jnp.float32)
             + jnp.einsum('bqhp,bkp->bhqk', qpb, kpe, preferred_element_type=jnp.float32)) * SM_SCALE
        qpos = i * Q_BLOCK + jnp.arange(Q_BLOCK)
        s = jnp.where(kpos[None, :] <= qpos[:, None], s, -jnp.inf)
        p = jax.nn.softmax(s, axis=-1).astype(ckv.dtype)
        return jnp.einsum('bhqk,bkr->bqhr', p, ckv)

    o = lax.map(block, (jnp.arange(nb), ql, qp))
    return o.transpose(1, 0, 2, 3, 4).reshape(B, S, H, R)


def mla_attend_sample(q_lat, q_pe, ckv_new, kpe_new, ckv_past, kpe_past):
    T = q_lat.shape[1]
    n_past = ckv_past.shape[1]
    s_past = (jnp.einsum('bqhr,bkr->bhqk', q_lat, ckv_past, preferred_element_type=jnp.float32)
              + jnp.einsum('bqhp,bkp->bhqk', q_pe, kpe_past, preferred_element_type=jnp.float32)) * SM_SCALE
    s_new = (jnp.einsum('bqhr,bkr->bhqk', q_lat, ckv_new, preferred_element_type=jnp.float32)
             + jnp.einsum('bqhp,bkp->bhqk', q_pe, kpe_new, preferred_element_type=jnp.float32)) * SM_SCALE
    causal = jnp.arange(T)[None, :] <= jnp.arange(T)[:, None]
    s_new = jnp.where(causal, s_new, -jnp.inf)
    p = jax.nn.softmax(jnp.concatenate([s_past, s_new], axis=-1), axis=-1).astype(ckv_new.dtype)
    return (jnp.einsum('bhqk,bkr->bqhr', p[..., :n_past], ckv_past)
            + jnp.einsum('bhqk,bkr->bqhr', p[..., n_past:], ckv_new))


def setup_inputs(seed: int = 0) -> dict:
    key = jax.random.key(seed)
    ks = jax.random.split(key, 32)
    n_pages = PAST_LEN // PAGE_SIZE
    n_pool_pages = (DEC_BATCH * n_pages * 5) // 4
    f32 = jnp.float32

    def nrm(k, shape, scale=1.0):
        return jax.random.normal(k, shape, f32) * scale

    def gain(k, shape):
        return 1.0 + 0.05 * jax.random.normal(k, shape, f32)

    page_table = jax.random.permutation(ks[0], n_pool_pages)[:DEC_BATCH * n_pages]
    page_table = page_table.reshape(DEC_BATCH, n_pages).astype(jnp.int32)
    return {
        'x_prompt': nrm(ks[1], (BATCH, SEQ, D_MODEL)),
        'x_sample': nrm(ks[2], (DEC_BATCH, DEC_SEQ, D_MODEL)),
        'cache_ckv': nrm(ks[3], (DEPTH, n_pool_pages, PAGE_SIZE, MLA_KV_RANK)),
        'cache_kpe': nrm(ks[4], (DEPTH, n_pool_pages, PAGE_SIZE, MLA_ROPE)),
        'page_table': page_table,
        'state_pool': nrm(ks[5], (DEPTH, DEC_BATCH, POOL_STATE, POOL_CH)),
        'state_conv': nrm(ks[6], (DEPTH, DEC_BATCH, CONV_STATE, CONV_CH), 0.5),
        'g_mix_norm': gain(ks[7], (DEPTH, D_MODEL)),
        'w_in': nrm(ks[8], (DEPTH, D_MODEL, IN_WIDTH), D_MODEL ** -0.5),
        'w_pool': nrm(ks[9], (DEPTH, POOL_GROUPS, POOL_GROUP_DIM, POOL_GROUP_DIM), POOL_GROUP_DIM ** -0.5),
        'pool_scale': gain(ks[10], (DEPTH, POOL_CH)),
        'g_q_norm': gain(ks[11], (DEPTH, MLA_Q_RANK)),
        'w_uq': nrm(ks[12], (DEPTH, MLA_Q_RANK, MLA_HEADS, MLA_NOPE + MLA_ROPE), MLA_Q_RANK ** -0.5),
        'g_kv_norm': gain(ks[13], (DEPTH, MLA_KV_RANK)),
        'w_uk': nrm(ks[14], (DEPTH, MLA_HEADS, MLA_KV_RANK, MLA_NOPE), MLA_KV_RANK ** -0.5),
        'w_uv': nrm(ks[15], (DEPTH, MLA_HEADS, MLA_KV_RANK, MLA_V_DIM), MLA_KV_RANK ** -0.5),
        'w_dw': nrm(ks[16], (DEPTH, DW_CONV_LEN, CONV_CH), DW_CONV_LEN ** -0.5),
        'b_dw': nrm(ks[17], (DEPTH, CONV_CH), 0.02),
        'g_conv_ln': gain(ks[18], (DEPTH, CONV_CH)),
        'b_conv_ln': nrm(ks[19], (DEPTH, CONV_CH), 0.02),
        'w_conv_pw': nrm(ks[20], (DEPTH, CONV_CH, CONV_CH), CONV_CH ** -0.5),
        'w_out': nrm(ks[21], (DEPTH, MIX_WIDTH, D_MODEL), MIX_WIDTH ** -0.5),
        'g_ffn_norm': gain(ks[22], (DEPTH, D_MODEL)),
        'w_gate': nrm(ks[23], (DEPTH, D_MODEL, D_FF), D_MODEL ** -0.5),
        'w_up': nrm(ks[24], (DEPTH, D_MODEL, D_FF), D_MODEL ** -0.5),
        'w_down': nrm(ks[25], (DEPTH, D_FF, D_MODEL), D_FF ** -0.5),
        'g_final': gain(ks[26], (D_MODEL,)),
    }


def reference(x_prompt, x_sample, cache_ckv, cache_kpe, page_table, state_pool, state_conv,
              g_mix_norm, w_in, w_pool, pool_scale, g_q_norm, w_uq, g_kv_norm, w_uk, w_uv,
              w_dw, b_dw, g_conv_ln, b_conv_ln, w_conv_pw, w_out, g_ffn_norm, w_gate, w_up,
              w_down, g_final):

    def trunk(x, start, pool_prefix, conv_prefix, attend):
        B, T, _ = x.shape
        cos, sin = rope_angles(start + jnp.arange(T))
        h = x
        ckvs, kpes, pools, convs = [], [], [], []
        for l in range(DEPTH):
            hn = rmsnorm(h, g_mix_norm[l])
            z = hn @ w_in[l]
            u_pool, c_q, c_kv, k_pe, conv_in = jnp.split(z, SPLIT_POINTS, axis=-1)
            pool_out, pool_st = pool_mixer(u_pool, pool_prefix[l], start, w_pool[l], pool_scale[l])
            q_lat, q_pe, ckv, kpe = mla_project(c_q, c_kv, k_pe, cos, sin, g_q_norm[l], w_uq[l],
                                                g_kv_norm[l], w_uk[l])
            o_lat = attend(l, q_lat, q_pe, ckv, kpe)
            mla_out = jnp.einsum('bshr,hrv->bshv', o_lat, w_uv[l]).reshape(B, T, MLA_CH)
            conv_out, conv_st = conv_mixer(conv_in, conv_prefix[l], w_dw[l], b_dw[l], g_conv_ln[l],
                                           b_conv_ln[l], w_conv_pw[l])
            h = h + jnp.concatenate([pool_out, mla_out, conv_out], axis=-1) @ w_out[l]
            hn = rmsnorm(h, g_ffn_norm[l])
            h = h + (jax.nn.silu(hn @ w_gate[l]) * (hn @ w_up[l])) @ w_down[l]
            ckvs.append(ckv)
            kpes.append(kpe)
            pools.append(pool_st)
            convs.append(conv_st)
        return (rmsnorm(h, g_final), jnp.stack(ckvs), jnp.stack(kpes), jnp.stack(pools),
                jnp.stack(convs))

    def attend_prompt(l, q_lat, q_pe, ckv, kpe):
        return mla_attend_prompt(q_lat, q_pe, ckv, kpe)

    dec_batch = page_table.shape[0]
    past_len = page_table.shape[1] * cache_ckv.shape[2]

    def attend_sample(l, q_lat, q_pe, ckv, kpe):
        ckv_past = cache_ckv[l, page_table].reshape(dec_batch, past_len, MLA_KV_RANK)
        kpe_past = cache_kpe[l, page_table].reshape(dec_batch, past_len, MLA_ROPE)
        return mla_attend_sample(q_lat, q_pe, ckv, kpe, ckv_past, kpe_past)

    bp = x_prompt.shape[0]
    zeros_pool = jnp.zeros((DEPTH, bp, POOL_STATE, POOL_CH), x_prompt.dtype)
    zeros_conv = jnp.zeros((DEPTH, bp, CONV_STATE, CONV_CH), x_prompt.dtype)
    y_prompt, ckv_prompt, kpe_prompt, pool_prompt, conv_prompt = trunk(
        x_prompt, 0, zeros_pool, zeros_conv, attend_prompt)
    y_sample, ckv_sample, kpe_sample, pool_sample, conv_sample = trunk(
        x_sample, past_len, state_pool, state_conv, attend_sample)
    return (y_prompt, y_sample, ckv_prompt, kpe_prompt, pool_prompt, conv_prompt,
            ckv_sample, kpe_sample, pool_sample, conv_sample)
```

```python
import functools

import jax
import jax.numpy as jnp
from jax import lax
from jax.experimental import pallas as pl
from jax.experimental.pallas import tpu as pltpu

F32 = jnp.float32
BF16 = jnp.bfloat16

POOL_WINDOWS = (2, 4, 8, 16)
POOL_STATE = max(POOL_WINDOWS) - 1
MLA_HEADS = 8
MLA_NOPE = 64
MLA_ROPE = 32
ROPE_THETA = 10000.0
SM_SCALE = (MLA_NOPE + MLA_ROPE) ** -0.5
DW_CONV_LEN = 31
CONV_STATE = DW_CONV_LEN - 1
NORM_EPS = 1e-6
LN_EPS = 1e-5

LANES = 128
SUBLANES = 8
VMEM_LIMIT_BYTES = 56 * 1024 * 1024

POOL_HALO = 16
CONV_HALO = 32

HEAD_PAIRS = MLA_HEADS // 2
ROPE_GROUPS = LANES // MLA_ROPE


def _rms(x, g):
    return x * lax.rsqrt(jnp.mean(x * x, axis=-1, keepdims=True) + NORM_EPS) * g


def _dot(a, b):
    return jnp.dot(a, b, preferred_element_type=F32)


def _dot_nt(a, b):
    return lax.dot_general(a, b, (((1,), (1,)), ((), ())), preferred_element_type=F32)


def _rope(x, cos, sin_signed):
    width = x.shape[-1]
    half = MLA_ROPE // 2
    lane = lax.broadcasted_iota(jnp.int32, x.shape, x.ndim - 1)
    first_half = (lane % MLA_ROPE) < half
    swapped = jnp.where(first_half, pltpu.roll(x, width - half, x.ndim - 1),
                        pltpu.roll(x, half, x.ndim - 1))
    return x * cos + swapped * sin_signed


def _pool_select(s2, s4, s8, s16):
    lane = lax.broadcasted_iota(jnp.int32, s2.shape, 1)
    group = s2.shape[1] // len(POOL_WINDOWS)
    return jnp.where(lane < group, s2,
                     jnp.where(lane < 2 * group, s4, jnp.where(lane < 3 * group, s8, s16)))


def _pool_window_lanes(shape):
    lane = lax.broadcasted_iota(jnp.int32, shape, 1)
    group = shape[1] // len(POOL_WINDOWS)
    w = jnp.where(lane < group, POOL_WINDOWS[0],
                  jnp.where(lane < 2 * group, POOL_WINDOWS[1],
                            jnp.where(lane < 3 * group, POOL_WINDOWS[2], POOL_WINDOWS[3])))
    return w


def _conv_tail(y, b_dw, g_ln, b_ln, w_pw):
    y = y + b_dw
    mu = jnp.mean(y, axis=-1, keepdims=True)
    yc = y - mu
    var = jnp.mean(yc * yc, axis=-1, keepdims=True)
    yn = yc * lax.rsqrt(var + LN_EPS) * g_ln + b_ln
    act = yn * jax.nn.sigmoid(yn)
    return _dot(act.astype(BF16), w_pw)


def _mix_in_kernel(h_ref, cos_ref, sin_ref, gmix_ref, win_ref, wpool_ref, pscale_ref, gq_ref,
                   wq_ref, wuk_ref, gkv_ref, wdw_ref, bdw_ref, gln_ref, bln_ref, wpw_ref,
                   *rest, decode, start, block_rows):
    if decode:
        (pstate_ref, cstate_ref, pool_out_ref, conv_out_ref, q_ref, kc_ref, ckv_ref, kpe_ref,
         unew_ref, gnew_ref) = rest
    else:
        (pool_out_ref, conv_out_ref, q_ref, kc_ref, ckv_ref, kpe_ref, pst_ref, cst_ref,
         pext_ref, cext_ref) = rest
    tm = block_rows
    pool_ch = wpool_ref.shape[0]
    conv_ch = wpw_ref.shape[0]
    kv_rank = gkv_ref.shape[1]
    q_rank = gq_ref.shape[1]
    c_pool, c_kv = 0, pool_ch
    c_q = c_kv + kv_rank
    c_kpe = c_q + q_rank
    c_val = c_kpe + LANES
    c_gate = c_val + conv_ch

    hn = _rms(h_ref[0], gmix_ref[...]).astype(BF16)
    cos = cos_ref[...]
    sin = sin_ref[...]

    u = _dot(hn, win_ref[:, c_pool:c_pool + pool_ch])
    if decode:
        unew_ref[...] = u
        s2 = u + pstate_ref[POOL_STATE - 1]
        s4 = s2 + pstate_ref[POOL_STATE - 2] + pstate_ref[POOL_STATE - 3]
        s8 = s4
        for j in range(4, 8):
            s8 = s8 + pstate_ref[POOL_STATE - j]
        s16 = s8
        for j in range(8, 16):
            s16 = s16 + pstate_ref[POOL_STATE - j]
        wsum = _pool_select(s2, s4, s8, s16)
        cnt = jnp.minimum(start + 1, _pool_window_lanes(u.shape)).astype(F32)
    else:
        t = pl.program_id(1)

        @pl.when(t == 0)
        def _():
            pext_ref[0:POOL_HALO, :] = jnp.zeros((POOL_HALO, pool_ch), F32)
            cext_ref[0:CONV_HALO, :] = jnp.zeros((CONV_HALO, conv_ch), F32)

        pext_ref[POOL_HALO:POOL_HALO + tm, :] = u
        ext = pext_ref[...]
        s2e = ext + pltpu.roll(ext, 1, 0)
        s4e = s2e + pltpu.roll(s2e, 2, 0)
        s8e = s4e + pltpu.roll(s4e, 4, 0)
        s16e = s8e + pltpu.roll(s8e, 8, 0)
        wsum = _pool_select(s2e[POOL_HALO:], s4e[POOL_HALO:], s8e[POOL_HALO:], s16e[POOL_HALO:])
        pos = start + t * tm + lax.broadcasted_iota(jnp.int32, u.shape, 0)
        cnt = jnp.minimum(pos + 1, _pool_window_lanes(u.shape)).astype(F32)
        pext_ref[0:POOL_HALO, :] = pext_ref[tm:tm + POOL_HALO, :]
        pst_ref[0] = pext_ref[0:POOL_HALO, :]
    d = wsum / cnt - u
    pool_out_ref[0] = (_dot(d.astype(BF16), wpool_ref[...]) * pscale_ref[...]).astype(BF16)

    val = _dot(hn, win_ref[:, c_val:c_val + conv_ch])
    gate = _dot(hn, win_ref[:, c_gate:c_gate + conv_ch])
    g = val * jax.nn.sigmoid(gate)
    if decode:
        gnew_ref[...] = g
        y = g * wdw_ref[CONV_STATE:CONV_STATE + 1, :]
        for k in range(CONV_STATE):
            y = y + cstate_ref[k] * wdw_ref[k:k + 1, :]
    else:
        cext_ref[CONV_HALO:CONV_HALO + tm, :] = g
        base = CONV_HALO - CONV_STATE
        y = cext_ref[base:base + tm, :] * wdw_ref[0:1, :]
        for k in range(1, DW_CONV_LEN):
            y = y + cext_ref[base + k:base + k + tm, :] * wdw_ref[k:k + 1, :]
        cext_ref[0:CONV_HALO, :] = cext_ref[tm:tm + CONV_HALO, :]
        cst_ref[0] = cext_ref[0:CONV_HALO, :]
    conv_out_ref[0] = _conv_tail(y, bdw_ref[...], gln_ref[...], bln_ref[...],
                                 wpw_ref[...]).astype(BF16)

    ckv = _rms(_dot(hn, win_ref[:, c_kv:c_kv + kv_rank]), gkv_ref[...])
    kpe4 = _rope(_dot(hn, win_ref[:, c_kpe:c_kpe + LANES]), cos[:, :LANES], sin[:, :LANES])
    ckv_ref[0] = ckv
    kpe_ref[0] = kpe4[:, :MLA_ROPE]
    kc_ref[0, :, 0:kv_rank] = ckv.astype(BF16)
    kc_ref[0, :, kv_rank:kv_rank + LANES] = kpe4.astype(BF16)

    cq = _rms(_dot(hn, win_ref[:, c_q:c_q + q_rank]), gq_ref[...]).astype(BF16)
    nope_w = MLA_HEADS * MLA_NOPE
    q_nope = _dot(cq, wq_ref[:, 0:nope_w]).astype(BF16)
    pe_w = wq_ref.shape[1] - nope_w
    if decode:
        cos_q = jnp.concatenate([cos[:, :LANES]] * MLA_HEADS, axis=1)
        sin_q = jnp.concatenate([sin[:, :LANES]] * MLA_HEADS, axis=1)
    else:
        cos_q, sin_q = cos, sin
    q_pe = _rope(_dot(cq, wq_ref[:, nope_w:nope_w + pe_w]), cos_q, sin_q) * SM_SCALE
    lane = lax.broadcasted_iota(jnp.int32, (tm, LANES), 1)
    for p in range(HEAD_PAIRS):
        q_lat2 = _dot(q_nope[:, 2 * p * MLA_NOPE:(2 * p + 2) * MLA_NOPE], wuk_ref[p]) * SM_SCALE
        for i in range(2):
            hd = 2 * p + i
            q_ref[0, hd, :, 0:kv_rank] = q_lat2[:, i * kv_rank:(i + 1) * kv_rank].astype(BF16)
            if decode:
                pe = q_pe[:, hd * LANES:(hd + 1) * LANES]
            else:
                grp = hd // ROPE_GROUPS
                pe = jnp.where(lane // MLA_ROPE == hd % ROPE_GROUPS,
                               q_pe[:, grp * LANES:(grp + 1) * LANES], 0.0)
            q_ref[0, hd, :, kv_rank:kv_rank + LANES] = pe.astype(BF16)


def _flash_kernel(q_ref, kc_ref, o_ref, m_ref, l_ref, acc_ref, *, tq, tk, kv_rank):
    i = pl.program_id(1)
    j = pl.program_id(2)
    rows = MLA_HEADS * tq
    last_j = ((i + 1) * tq - 1) // tk

    @pl.when(j == 0)
    def _():
        m_ref[...] = jnp.full(m_ref.shape, -jnp.inf, F32)
        l_ref[...] = jnp.zeros(l_ref.shape, F32)
        acc_ref[...] = jnp.zeros(acc_ref.shape, F32)

    def step(masked):
        q = q_ref[0].reshape(rows, q_ref.shape[-1])
        k = kc_ref[0]
        s = _dot_nt(q, k)
        if masked:
            qpos = i * tq + lax.broadcasted_iota(jnp.int32, (MLA_HEADS, tq, tk), 1).reshape(rows, tk)
            kpos = j * tk + lax.broadcasted_iota(jnp.int32, (rows, tk), 1)
            s = jnp.where(kpos <= qpos, s, -jnp.inf)
        m_prev = m_ref[...]
        m_new = jnp.maximum(m_prev, jnp.max(s, axis=1, keepdims=True))
        alpha = jnp.exp(m_prev - m_new)
        p = jnp.exp(s - m_new[:, 0:1])
        l_ref[...] = alpha * l_ref[...] + jnp.sum(p, axis=1, keepdims=True)
        acc_ref[...] = alpha[:, 0:1] * acc_ref[...] + _dot(p.astype(BF16), k[:, 0:kv_rank])
        m_ref[...] = m_new

    needs_mask = (j + 1) * tk - 1 > i * tq

    @pl.when(jnp.logical_and(j <= last_j, jnp.logical_not(needs_mask)))
    def _():
        step(False)

    @pl.when(jnp.logical_and(j <= last_j, needs_mask))
    def _():
        step(True)

    @pl.when(j == last_j)
    def _():
        o = acc_ref[...] / l_ref[...][:, 0:1]
        o_ref[0] = o.reshape(MLA_HEADS, tq, kv_rank).astype(o_ref.dtype)


def _paged_kernel(pt_ref, q_ref, kcn_ref, ckv_hbm, kpe_hbm, o_ref, ckv_buf, kpe_buf, sems,
                  *, layer, n_pages, kv_rank):
    b = pl.program_id(0)
    nb = pl.num_programs(0)

    def page_copies(seq, slot, p):
        page = pt_ref[seq * n_pages + p]
        return (pltpu.make_async_copy(ckv_hbm.at[layer, page], ckv_buf.at[slot, p], sems.at[0, slot]),
                pltpu.make_async_copy(kpe_hbm.at[layer, page], kpe_buf.at[slot, p], sems.at[1, slot]))

    def start_seq(seq, slot):
        def body(p, c):
            for cp in page_copies(seq, slot, p):
                cp.start()
            return c
        lax.fori_loop(0, n_pages, body, 0)

    def wait_seq(seq, slot):
        def body(p, c):
            for cp in page_copies(seq, slot, p):
                cp.wait()
            return c
        lax.fori_loop(0, n_pages, body, 0)

    slot = b % 2

    @pl.when(b == 0)
    def _():
        start_seq(0, 0)

    @pl.when(b + 1 < nb)
    def _():
        start_seq(b + 1, 1 - slot)

    wait_seq(b, slot)

    page_rows = ckv_buf.shape[2]
    n_keys = n_pages * page_rows
    q = q_ref[0]
    ck = ckv_buf[slot].reshape(n_keys, kv_rank).astype(BF16)
    kp = kpe_buf[slot].reshape(n_keys, MLA_ROPE).astype(BF16)
    s = _dot_nt(q[:, 0:kv_rank], ck) + _dot_nt(q[:, kv_rank:kv_rank + MLA_ROPE], kp)
    kcn = kcn_ref[0]
    s_new = jnp.sum(q.astype(F32) * kcn.astype(F32), axis=1, keepdims=True)
    m = jnp.maximum(jnp.max(s, axis=1, keepdims=True), s_new)
    p = jnp.exp(s - m)
    p_new = jnp.exp(s_new - m)
    l = jnp.sum(p, axis=1, keepdims=True) + p_new
    v_new = kcn[:, 0:kv_rank].astype(F32)
    p_new_r = p_new.astype(BF16).astype(F32)
    o = _dot(p.astype(BF16), ck) + p_new_r * v_new
    o_ref[0] = (o / l).astype(o_ref.dtype)


def _mix_out_kernel(h_ref, pool_ref, olat_ref, conv_ref, wuv_ref, wo_ref, gffn_ref, wg_ref, wu_ref,
                    wd_ref, gfin_ref, out_ref, act_ref, *, final, ff_chunks):
    pairs = []
    for p in range(HEAD_PAIRS):
        o2 = jnp.concatenate([olat_ref[0, 2 * p], olat_ref[0, 2 * p + 1]], axis=-1)
        pairs.append(_dot(o2, wuv_ref[p]).astype(BF16))
    mixed = jnp.concatenate([pool_ref[0]] + pairs + [conv_ref[0]], axis=-1)
    h1 = h_ref[0] + _dot(mixed, wo_ref[...])
    hn = _rms(h1, gffn_ref[...]).astype(BF16)
    for c0, c1 in ff_chunks:
        gt = _dot(hn, wg_ref[:, c0:c1])
        up = _dot(hn, wu_ref[:, c0:c1])
        act_ref[:, c0:c1] = (gt * jax.nn.sigmoid(gt) * up).astype(BF16)
    h2 = h1 + _dot(act_ref[...], wd_ref[...])
    if final:
        h2 = _rms(h2, gfin_ref[...])
    out_ref[0] = h2


def _const_spec(shape):
    nd = len(shape)
    return pl.BlockSpec(shape, lambda *_: (0,) * nd, pipeline_mode=pl.Buffered(1))


def _params(n_axes):
    return pltpu.CompilerParams(dimension_semantics=("arbitrary",) * n_axes,
                                vmem_limit_bytes=VMEM_LIMIT_BYTES)


def _row_block(t, target):
    tm = min(t, target)
    assert t % tm == 0 and tm % 16 == 0, (t, tm)
    return tm


def _mix_in_prompt(h, cos, sin, lw, start):
    b, t, d = h.shape
    tm = _row_block(t, 512)
    pool_ch, conv_ch = lw["w_pool"].shape[0], lw["w_pw"].shape[0]
    kv_rank = lw["g_kv"].shape[1]
    qk_w = kv_rank + LANES
    weights = [lw[k] for k in ("g_mix", "w_in", "w_pool", "pool_scale", "g_q", "w_q", "w_uk",
                               "g_kv", "w_dw", "b_dw", "g_ln", "b_ln", "w_pw")]
    row = lambda w: pl.BlockSpec((1, tm, w), lambda bi, ti: (bi, ti, 0))
    in_specs = ([row(d), pl.BlockSpec((tm, cos.shape[1]), lambda bi, ti: (ti, 0)),
                 pl.BlockSpec((tm, sin.shape[1]), lambda bi, ti: (ti, 0))]
                + [_const_spec(w.shape) for w in weights])
    out_shape = (jax.ShapeDtypeStruct((b, t, pool_ch), BF16),
                 jax.ShapeDtypeStruct((b, t, conv_ch), BF16),
                 jax.ShapeDtypeStruct((b, MLA_HEADS, t, qk_w), BF16),
                 jax.ShapeDtypeStruct((b, t, qk_w), BF16),
                 jax.ShapeDtypeStruct((b, t, kv_rank), F32),
                 jax.ShapeDtypeStruct((b, t, MLA_ROPE), F32),
                 jax.ShapeDtypeStruct((b, POOL_HALO, pool_ch), F32),
                 jax.ShapeDtypeStruct((b, CONV_HALO, conv_ch), F32))
    out_specs = (row(pool_ch), row(conv_ch),
                 pl.BlockSpec((1, MLA_HEADS, tm, qk_w), lambda bi, ti: (bi, 0, ti, 0)),
                 row(qk_w), row(kv_rank), row(MLA_ROPE),
                 pl.BlockSpec((1, POOL_HALO, pool_ch), lambda bi, ti: (bi, 0, 0)),
                 pl.BlockSpec((1, CONV_HALO, conv_ch), lambda bi, ti: (bi, 0, 0)))
    return pl.pallas_call(
        functools.partial(_mix_in_kernel, decode=False, start=start, block_rows=tm),
        grid=(b, t // tm), in_specs=in_specs, out_specs=out_specs, out_shape=out_shape,
        scratch_shapes=[pltpu.VMEM((tm + POOL_HALO, pool_ch), F32),
                        pltpu.VMEM((tm + CONV_HALO, conv_ch), F32)],
        compiler_params=_params(2), name="mix_in_prompt",
    )(h, cos, sin, *weights)


def _mix_in_decode(h, cos, sin, lw, pool_state_t, conv_state_t, start):
    _, n, d = h.shape
    pool_ch, conv_ch = lw["w_pool"].shape[0], lw["w_pw"].shape[0]
    kv_rank = lw["g_kv"].shape[1]
    qk_w = kv_rank + LANES
    weights = [lw[k] for k in ("g_mix", "w_in", "w_pool", "pool_scale", "g_q", "w_q_dec", "w_uk",
                               "g_kv", "w_dw", "b_dw", "g_ln", "b_ln", "w_pw")]
    full = lambda shape: pl.BlockSpec(shape, lambda i: (0,) * len(shape))
    args = [h, cos, sin] + weights + [pool_state_t, conv_state_t]
    out_shape = (jax.ShapeDtypeStruct((1, n, pool_ch), BF16),
                 jax.ShapeDtypeStruct((1, n, conv_ch), BF16),
                 jax.ShapeDtypeStruct((1, MLA_HEADS, n, qk_w), BF16),
                 jax.ShapeDtypeStruct((1, n, qk_w), BF16),
                 jax.ShapeDtypeStruct((1, n, kv_rank), F32),
                 jax.ShapeDtypeStruct((1, n, MLA_ROPE), F32),
                 jax.ShapeDtypeStruct((n, pool_ch), F32),
                 jax.ShapeDtypeStruct((n, conv_ch), F32))
    return pl.pallas_call(
        functools.partial(_mix_in_kernel, decode=True, start=start, block_rows=n),
        grid=(1,), in_specs=[full(a.shape) for a in args],
        out_specs=tuple(full(s.shape) for s in out_shape), out_shape=out_shape,
        compiler_params=_params(1), name="mix_in_decode",
    )(*args)


def _flash_attention(q, kc, kv_rank):
    b, hds, t, qk_w = q.shape
    tq = _row_block(t, 256)
    tk = _row_block(t, 512)
    rows = hds * tq

    def kc_index(bi, i, j):
        return (bi, jnp.minimum(j, ((i + 1) * tq - 1) // tk), 0)

    return pl.pallas_call(
        functools.partial(_flash_kernel, tq=tq, tk=tk, kv_rank=kv_rank),
        grid=(b, t // tq, t // tk),
        in_specs=[pl.BlockSpec((1, hds, tq, qk_w), lambda bi, i, j: (bi, 0, i, 0)),
                  pl.BlockSpec((1, tk, qk_w), kc_index)],
        out_specs=pl.BlockSpec((1, hds, tq, kv_rank), lambda bi, i, j: (bi, 0, i, 0)),
        out_shape=jax.ShapeDtypeStruct((b, hds, t, kv_rank), BF16),
        scratch_shapes=[pltpu.VMEM((rows, LANES), F32), pltpu.VMEM((rows, LANES), F32),
                        pltpu.VMEM((rows, kv_rank), F32)],
        compiler_params=_params(3), name="flash_attention",
    )(q, kc)


def _paged_attention(page_table, q, kc_new, cache_ckv, cache_kpe, layer):
    n, hds, qk_w = q.shape
    n_pages = page_table.shape[1]
    page_rows, kv_rank = cache_ckv.shape[2], cache_ckv.shape[3]
    grid_spec = pltpu.PrefetchScalarGridSpec(
        num_scalar_prefetch=1, grid=(n,),
        in_specs=[pl.BlockSpec((1, hds, qk_w), lambda bi, pt: (bi, 0, 0)),
                  pl.BlockSpec((1, 1, qk_w), lambda bi, pt: (bi, 0, 0)),
                  pl.BlockSpec(memory_space=pl.ANY),
                  pl.BlockSpec(memory_space=pl.ANY)],
        out_specs=pl.BlockSpec((1, hds, kv_rank), lambda bi, pt: (bi, 0, 0)),
        scratch_shapes=[pltpu.VMEM((2, n_pages, page_rows, kv_rank), F32),
                        pltpu.VMEM((2, n_pages, page_rows, MLA_ROPE), F32),
                        pltpu.SemaphoreType.DMA((2, 2))])
    return pl.pallas_call(
        functools.partial(_paged_kernel, layer=layer, n_pages=n_pages, kv_rank=kv_rank),
        grid_spec=grid_spec, out_shape=jax.ShapeDtypeStruct((n, hds, kv_rank), BF16),
        compiler_params=_params(1), name="paged_attention",
    )(page_table.reshape(-1), q, kc_new, cache_ckv, cache_kpe)


def _ff_chunks(d_ff, width=512):
    return tuple((c, min(c + width, d_ff)) for c in range(0, d_ff, width))


def _mix_out(h, pool_out, o_lat, conv_out, lw, g_final, final):
    b, t, d = h.shape
    tm = _row_block(t, 512)
    pool_ch, conv_ch = pool_out.shape[2], conv_out.shape[2]
    kv_rank = o_lat.shape[3]
    d_ff = lw["w_gate"].shape[1]
    weights = [lw[k] for k in ("w_uv", "w_out", "g_ffn", "w_gate", "w_up", "w_down")] + [g_final]
    row = lambda w: pl.BlockSpec((1, tm, w), lambda bi, ti: (bi, ti, 0))
    in_specs = ([row(d), row(pool_ch),
                 pl.BlockSpec((1, MLA_HEADS, tm, kv_rank), lambda bi, ti: (bi, 0, ti, 0)),
                 row(conv_ch)] + [_const_spec(w.shape) for w in weights])
    return pl.pallas_call(
        functools.partial(_mix_out_kernel, final=final, ff_chunks=_ff_chunks(d_ff)),
        grid=(b, t // tm), in_specs=in_specs, out_specs=row(d),
        out_shape=jax.ShapeDtypeStruct((b, t, d), F32),
        scratch_shapes=[pltpu.VMEM((tm, d_ff), BF16)],
        compiler_params=_params(2), name="mix_out",
    )(h, pool_out, o_lat, conv_out, *weights)


def _rope_tables(start, t):
    inv = ROPE_THETA ** (-jnp.arange(0, MLA_ROPE, 2, dtype=F32) / MLA_ROPE)
    ang = (start + jnp.arange(t)).astype(F32)[:, None] * inv[None, :]
    cos, sin = jnp.cos(ang), jnp.sin(ang)
    reps = MLA_HEADS
    return (jnp.tile(jnp.concatenate([cos, cos], axis=1), (1, reps)),
            jnp.tile(jnp.concatenate([-sin, sin], axis=1), (1, reps)))


def _block_diag(blocks):
    n, r, c = blocks.shape[-3:]
    eye = jnp.eye(n, dtype=blocks.dtype)
    out = blocks[..., :, :, None, :] * eye[:, None, :, None]
    return out.reshape(blocks.shape[:-3] + (n * r, n * c))


def _prepare_weights(g_mix_norm, w_in, w_pool, pool_scale, g_q_norm, w_uq, g_kv_norm, w_uk, w_uv,
                     w_dw, b_dw, g_conv_ln, b_conv_ln, w_conv_pw, w_out, g_ffn_norm, w_gate, w_up,
                     w_down):
    depth = w_in.shape[0]
    pool_ch = pool_scale.shape[1]
    q_rank = g_q_norm.shape[1]
    kv_rank = g_kv_norm.shape[1]
    conv_ch = w_conv_pw.shape[1]
    s0 = pool_ch
    s1 = s0 + q_rank
    s2 = s1 + kv_rank
    s3 = s2 + MLA_ROPE
    w_in_p = jnp.concatenate(
        [w_in[:, :, 0:s0], w_in[:, :, s1:s2], w_in[:, :, s0:s1],
         jnp.tile(w_in[:, :, s2:s3], (1, 1, ROPE_GROUPS)),
         w_in[:, :, s3:s3 + conv_ch], w_in[:, :, s3 + conv_ch:s3 + 2 * conv_ch]],
        axis=2).astype(BF16)
    nope = w_uq[..., :MLA_NOPE].reshape(depth, q_rank, MLA_HEADS * MLA_NOPE)
    pe = w_uq[..., MLA_NOPE:]
    pe_dense = pe.reshape(depth, q_rank, MLA_HEADS * MLA_ROPE)
    pe_pad = jnp.pad(pe, ((0, 0), (0, 0), (0, 0), (0, LANES - MLA_ROPE)))
    pe_pad = pe_pad.reshape(depth, q_rank, MLA_HEADS * LANES)
    w_q = jnp.concatenate([nope, pe_dense], axis=2).astype(BF16)
    w_q_dec = jnp.concatenate([nope, pe_pad], axis=2).astype(BF16)
    w_uk_t = jnp.swapaxes(w_uk, 2, 3).reshape(depth, HEAD_PAIRS, 2, MLA_NOPE, kv_rank)
    w_uk_p = _block_diag(w_uk_t).astype(BF16)
    w_uv_p = _block_diag(w_uv.reshape(depth, HEAD_PAIRS, 2, kv_rank, w_uv.shape[3])).astype(BF16)
    vec = lambda a: a.reshape(depth, 1, a.shape[1])
    stacked = dict(
        g_mix=vec(g_mix_norm), w_in=w_in_p, w_pool=_block_diag(w_pool).astype(BF16),
        pool_scale=vec(pool_scale), g_q=vec(g_q_norm), w_q=w_q, w_q_dec=w_q_dec, w_uk=w_uk_p,
        g_kv=vec(g_kv_norm), w_dw=w_dw, b_dw=vec(b_dw), g_ln=vec(g_conv_ln), b_ln=vec(b_conv_ln),
        w_pw=w_conv_pw.astype(BF16), w_uv=w_uv_p, w_out=w_out.astype(BF16), g_ffn=vec(g_ffn_norm),
        w_gate=w_gate.astype(BF16), w_up=w_up.astype(BF16), w_down=w_down.astype(BF16))
    return [{k: v[l] for k, v in stacked.items()} for l in range(depth)]


def kernel(x_prompt, x_sample, cache_ckv, cache_kpe, page_table, state_pool, state_conv, g_mix_norm, w_in, w_pool, pool_scale, g_q_norm, w_uq, g_kv_norm, w_uk, w_uv, w_dw, b_dw, g_conv_ln, b_conv_ln, w_conv_pw, w_out, g_ffn_norm, w_gate, w_up, w_down, g_final):
    depth = w_in.shape[0]
    layers = _prepare_weights(g_mix_norm, w_in, w_pool, pool_scale, g_q_norm, w_uq, g_kv_norm, w_uk,
                              w_uv, w_dw, b_dw, g_conv_ln, b_conv_ln, w_conv_pw, w_out, g_ffn_norm,
                              w_gate, w_up, w_down)
    g_fin = g_final.reshape(1, -1)
    kv_rank = g_kv_norm.shape[1]

    seq = x_prompt.shape[1]
    cos_p, sin_p = _rope_tables(0, seq)
    h = x_prompt
    ckvs, kpes, pools, convs = [], [], [], []
    for l, lw in enumerate(layers):
        pool_out, conv_out, q, kc, ckv, kpe, pst, cst = _mix_in_prompt(h, cos_p, sin_p, lw, 0)
        o_lat = _flash_attention(q, kc, kv_rank)
        h = _mix_out(h, pool_out, o_lat, conv_out, lw, g_fin, l == depth - 1)
        ckvs.append(ckv)
        kpes.append(kpe)
        pools.append(pst[:, POOL_HALO - POOL_STATE:])
        convs.append(cst[:, CONV_HALO - CONV_STATE:])
    y_prompt = h
    ckv_prompt, kpe_prompt = jnp.stack(ckvs), jnp.stack(kpes)
    pool_prompt, conv_prompt = jnp.stack(pools), jnp.stack(convs)

    n_dec, dec_seq, d_model = x_sample.shape
    assert dec_seq == 1, "the sample path handles one new token per sequence"
    past_len = page_table.shape[1] * cache_ckv.shape[2]
    cos_s, sin_s = _rope_tables(past_len, 1)
    cos_s = jnp.broadcast_to(cos_s, (n_dec, cos_s.shape[1]))
    sin_s = jnp.broadcast_to(sin_s, (n_dec, sin_s.shape[1]))
    pool_t = jnp.swapaxes(state_pool, 1, 2)
    conv_t = jnp.swapaxes(state_conv, 1, 2)
    h = x_sample.reshape(1, n_dec, d_model)
    ckvs, kpes, pools, convs = [], [], [], []
    for l, lw in enumerate(layers):
        pool_out, conv_out, q, kc, ckv, kpe, u_new, g_new = _mix_in_decode(
            h, cos_s, sin_s, lw, pool_t[l], conv_t[l], past_len)
        o_lat = _paged_attention(page_table, jnp.swapaxes(q[0], 0, 1), kc.reshape(n_dec, 1, -1),
                                 cache_ckv, cache_kpe, l)
        h = _mix_out(h, pool_out, jnp.swapaxes(o_lat, 0, 1)[None], conv_out, lw, g_fin,
                     l == depth - 1)
        ckvs.append(ckv.reshape(n_dec, 1, -1))
        kpes.append(kpe.reshape(n_dec, 1, -1))
        pools.append(jnp.concatenate([state_pool[l][:, 1:], u_new[:, None]], axis=1))
        convs.append(jnp.concatenate([state_conv[l][:, 1:], g_new[:, None]], axis=1))
    y_sample = h.reshape(n_dec, 1, d_model)
    return (y_prompt, y_sample, ckv_prompt, kpe_prompt, pool_prompt, conv_prompt,
            jnp.stack(ckvs), jnp.stack(kpes), jnp.stack(pools), jnp.stack(convs))
```

```python
import functools

import jax
import jax.numpy as jnp
from jax import lax
from jax.experimental import pallas as pl
from jax.experimental.pallas import tpu as pltpu

F32 = jnp.float32
BF16 = jnp.bfloat16

POOL_WINDOWS = (2, 4, 8, 16)
POOL_STATE = max(POOL_WINDOWS) - 1
MLA_HEADS = 8
MLA_NOPE = 64
MLA_ROPE = 32
ROPE_THETA = 10000.0
SM_SCALE = (MLA_NOPE + MLA_ROPE) ** -0.5
Q_SCALE = SM_SCALE * 1.4426950408889634
DW_CONV_LEN = 31
CONV_STATE = DW_CONV_LEN - 1
NORM_EPS = 1e-6
LN_EPS = 1e-5

LANES = 128
SUBLANES = 8
VMEM_LIMIT_BYTES = 56 * 1024 * 1024

POOL_HALO = 16
CONV_HALO = 32

HEAD_PAIRS = MLA_HEADS // 2
FLASH_CHUNK_HEADS = 1
ROPE_GROUPS = LANES // MLA_ROPE


def _rms(x, g):
    return x * lax.rsqrt(jnp.mean(x * x, axis=-1, keepdims=True) + NORM_EPS) * g


def _dot(a, b):
    return jnp.dot(a, b, preferred_element_type=F32)


def _dot_nt(a, b):
    return lax.dot_general(a, b, (((1,), (1,)), ((), ())), preferred_element_type=F32)


def _rope(x, cos, sin_signed):
    width = x.shape[-1]
    half = MLA_ROPE // 2
    lane = lax.broadcasted_iota(jnp.int32, x.shape, x.ndim - 1)
    first_half = (lane % MLA_ROPE) < half
    swapped = jnp.where(first_half, pltpu.roll(x, width - half, x.ndim - 1),
                        pltpu.roll(x, half, x.ndim - 1))
    return x * cos + swapped * sin_signed


def _pool_select(s2, s4, s8, s16):
    lane = lax.broadcasted_iota(jnp.int32, s2.shape, 1)
    group = s2.shape[1] // len(POOL_WINDOWS)
    return jnp.where(lane < group, s2,
                     jnp.where(lane < 2 * group, s4, jnp.where(lane < 3 * group, s8, s16)))


def _pool_window_lanes(shape):
    lane = lax.broadcasted_iota(jnp.int32, shape, 1)
    group = shape[1] // len(POOL_WINDOWS)
    w = jnp.where(lane < group, POOL_WINDOWS[0],
                  jnp.where(lane < 2 * group, POOL_WINDOWS[1],
                            jnp.where(lane < 3 * group, POOL_WINDOWS[2], POOL_WINDOWS[3])))
    return w


def _conv_tail(y, b_dw, g_ln, b_ln, w_pw):
    y = y + b_dw
    mu = jnp.mean(y, axis=-1, keepdims=True)
    yc = y - mu
    var = jnp.mean(yc * yc, axis=-1, keepdims=True)
    yn = yc * lax.rsqrt(var + LN_EPS) * g_ln + b_ln
    act = yn * jax.nn.sigmoid(yn)
    return _dot(act.astype(BF16), w_pw)


def _mix_in_kernel(h_ref, cos_ref, sin_ref, gmix_ref, win_ref, wpool_ref, pscale_ref, gq_ref,
                   wq_ref, wuk_ref, gkv_ref, wdw_ref, bdw_ref, gln_ref, bln_ref, wpw_ref,
                   *rest, decode, start, block_rows):
    if decode:
        (pstate_ref, cstate_ref, pool_out_ref, conv_out_ref, q_ref, kc_ref, ckv_ref, kpe_ref,
         unew_ref, gnew_ref) = rest
    else:
        (pool_out_ref, conv_out_ref, q_ref, kc_ref, ckv_ref, kpe_ref, pst_ref, cst_ref,
         pext_ref, cext_ref) = rest
    tm = block_rows
    pool_ch = wpool_ref.shape[0]
    conv_ch = wpw_ref.shape[0]
    kv_rank = gkv_ref.shape[1]
    q_rank = gq_ref.shape[1]
    c_pool, c_kv = 0, pool_ch
    c_q = c_kv + kv_rank
    c_kpe = c_q + q_rank
    c_val = c_kpe + LANES
    c_gate = c_val + conv_ch

    hn = _rms(h_ref[0], gmix_ref[...]).astype(BF16)
    cos = cos_ref[...]
    sin = sin_ref[...]

    u = _dot(hn, win_ref[:, c_pool:c_pool + pool_ch])
    if decode:
        unew_ref[...] = u
        s2 = u + pstate_ref[POOL_STATE - 1]
        s4 = s2 + pstate_ref[POOL_STATE - 2] + pstate_ref[POOL_STATE - 3]
        s8 = s4
        for j in range(4, 8):
            s8 = s8 + pstate_ref[POOL_STATE - j]
        s16 = s8
        for j in range(8, 16):
            s16 = s16 + pstate_ref[POOL_STATE - j]
        wsum = _pool_select(s2, s4, s8, s16)
        cnt = jnp.minimum(start + 1, _pool_window_lanes(u.shape)).astype(F32)
    else:
        t = pl.program_id(1)

        @pl.when(t == 0)
        def _():
            pext_ref[0:POOL_HALO, :] = jnp.zeros((POOL_HALO, pool_ch), F32)
            cext_ref[0:CONV_HALO, :] = jnp.zeros((CONV_HALO, conv_ch), F32)

        pext_ref[POOL_HALO:POOL_HALO + tm, :] = u
        ext = pext_ref[...]
        s2e = ext + pltpu.roll(ext, 1, 0)
        s4e = s2e + pltpu.roll(s2e, 2, 0)
        s8e = s4e + pltpu.roll(s4e, 4, 0)
        s16e = s8e + pltpu.roll(s8e, 8, 0)
        wsum = _pool_select(s2e[POOL_HALO:], s4e[POOL_HALO:], s8e[POOL_HALO:], s16e[POOL_HALO:])
        pos = start + t * tm + lax.broadcasted_iota(jnp.int32, u.shape, 0)
        cnt = jnp.minimum(pos + 1, _pool_window_lanes(u.shape)).astype(F32)
        pext_ref[0:POOL_HALO, :] = pext_ref[tm:tm + POOL_HALO, :]
        pst_ref[0] = pext_ref[0:POOL_HALO, :]
    d = wsum / cnt - u
    pool_out_ref[0] = (_dot(d.astype(BF16), wpool_ref[...]) * pscale_ref[...]).astype(BF16)

    val = _dot(hn, win_ref[:, c_val:c_val + conv_ch])
    gate = _dot(hn, win_ref[:, c_gate:c_gate + conv_ch])
    g = val * jax.nn.sigmoid(gate)
    if decode:
        gnew_ref[...] = g
        y = g * wdw_ref[CONV_STATE:CONV_STATE + 1, :]
        for k in range(CONV_STATE):
            y = y + cstate_ref[k] * wdw_ref[k:k + 1, :]
    else:
        cext_ref[CONV_HALO:CONV_HALO + tm, :] = g
        base = CONV_HALO - CONV_STATE
        y = cext_ref[base:base + tm, :] * wdw_ref[0:1, :]
        for k in range(1, DW_CONV_LEN):
            y = y + cext_ref[base + k:base + k + tm, :] * wdw_ref[k:k + 1, :]
        cext_ref[0:CONV_HALO, :] = cext_ref[tm:tm + CONV_HALO, :]
        cst_ref[0] = cext_ref[0:CONV_HALO, :]
    conv_out_ref[0] = _conv_tail(y, bdw_ref[...], gln_ref[...], bln_ref[...],
                                 wpw_ref[...]).astype(BF16)

    ckv = _rms(_dot(hn, win_ref[:, c_kv:c_kv + kv_rank]), gkv_ref[...])
    kpe4 = _rope(_dot(hn, win_ref[:, c_kpe:c_kpe + LANES]), cos[:, :LANES], sin[:, :LANES])
    ckv_ref[0] = ckv
    kpe_ref[0] = kpe4[:, :MLA_ROPE]
    kc_ref[0, :, 0:kv_rank] = ckv.astype(BF16)
    kc_ref[0, :, kv_rank:kv_rank + LANES] = kpe4.astype(BF16)

    cq = _rms(_dot(hn, win_ref[:, c_q:c_q + q_rank]), gq_ref[...]).astype(BF16)
    nope_w = MLA_HEADS * MLA_NOPE
    q_nope = _dot(cq, wq_ref[:, 0:nope_w]).astype(BF16)
    pe_w = wq_ref.shape[1] - nope_w
    if decode:
        cos_q = jnp.concatenate([cos[:, :LANES]] * MLA_HEADS, axis=1)
        sin_q = jnp.concatenate([sin[:, :LANES]] * MLA_HEADS, axis=1)
    else:
        cos_q, sin_q = cos, sin
    q_pe = _rope(_dot(cq, wq_ref[:, nope_w:nope_w + pe_w]), cos_q, sin_q) * Q_SCALE
    lane = lax.broadcasted_iota(jnp.int32, (tm, LANES), 1)
    for p in range(HEAD_PAIRS):
        q_lat2 = _dot(q_nope[:, 2 * p * MLA_NOPE:(2 * p + 2) * MLA_NOPE], wuk_ref[p]) * Q_SCALE
        for i in range(2):
            hd = 2 * p + i
            q_ref[0, hd, :, 0:kv_rank] = q_lat2[:, i * kv_rank:(i + 1) * kv_rank].astype(BF16)
            if decode:
                pe = q_pe[:, hd * LANES:(hd + 1) * LANES]
            else:
                grp = hd // ROPE_GROUPS
                pe = jnp.where(lane // MLA_ROPE == hd % ROPE_GROUPS,
                               q_pe[:, grp * LANES:(grp + 1) * LANES], 0.0)
            q_ref[0, hd, :, kv_rank:kv_rank + LANES] = pe.astype(BF16)


def _lane_repeat(x, width):
    return jnp.concatenate([x] * (width // LANES), axis=1)


def _flash_kernel(itab_ref, jtab_ref, q_ref, kc_ref, o_ref, m_ref, l_ref, acc_ref, *, tq, tk, kv_rank):
    step = pl.program_id(1)
    i = itab_ref[step]
    j = jtab_ref[step]
    rows = MLA_HEADS * tq

    @pl.when(j == 0)
    def _():
        m_ref[...] = jnp.full(m_ref.shape, -jnp.inf, F32)
        l_ref[...] = jnp.zeros(l_ref.shape, F32)
        acc_ref[...] = jnp.zeros(acc_ref.shape, F32)

    def update(masked):
        k = kc_ref[0]
        if masked:
            chunk_rows = FLASH_CHUNK_HEADS * tq
            qpos = i * tq + lax.broadcasted_iota(
                jnp.int32, (FLASH_CHUNK_HEADS, tq, tk), 1).reshape(chunk_rows, tk)
            kpos = j * tk + lax.broadcasted_iota(jnp.int32, (chunk_rows, tk), 1)
            allowed = kpos <= qpos
        for c in range(MLA_HEADS // FLASH_CHUNK_HEADS):
            hs = slice(c * FLASH_CHUNK_HEADS, (c + 1) * FLASH_CHUNK_HEADS)
            rs = slice(c * FLASH_CHUNK_HEADS * tq, (c + 1) * FLASH_CHUNK_HEADS * tq)
            q = q_ref[0, hs].reshape(FLASH_CHUNK_HEADS * tq, q_ref.shape[-1])
            s = _dot_nt(q, k)
            if masked:
                s = jnp.where(allowed, s, -jnp.inf)
            m_prev = m_ref[rs]
            m_new = jnp.maximum(m_prev, jnp.max(s, axis=1, keepdims=True))
            alpha = jnp.exp2(m_prev - m_new)
            p = jnp.exp2(s - _lane_repeat(m_new, tk))
            l_ref[rs] = alpha * l_ref[rs] + jnp.sum(p, axis=1, keepdims=True)
            acc_ref[rs] = (_lane_repeat(alpha, kv_rank) * acc_ref[rs]
                           + _dot(p.astype(BF16), k[:, 0:kv_rank]))
            m_ref[rs] = m_new

    needs_mask = (j + 1) * tk - 1 > i * tq

    @pl.when(jnp.logical_not(needs_mask))
    def _():
        update(False)

    @pl.when(needs_mask)
    def _():
        update(True)

    @pl.when((j + 1) * tk >= (i + 1) * tq)
    def _():
        o = acc_ref[...] / _lane_repeat(l_ref[...], kv_rank)
        o_ref[0] = o.reshape(MLA_HEADS, tq, kv_rank).astype(o_ref.dtype)


def _paged_kernel(pt_ref, q_ref, kcn_ref, ckv_hbm, kpe_hbm, o_ref, ckv_buf, kpe_buf, sems,
                  *, layer, n_pages, kv_rank):
    b = pl.program_id(0)
    nb = pl.num_programs(0)

    def page_copies(seq, slot, p):
        page = pt_ref[seq * n_pages + p]
        return (pltpu.make_async_copy(ckv_hbm.at[layer, page], ckv_buf.at[slot, p], sems.at[0, slot]),
                pltpu.make_async_copy(kpe_hbm.at[layer, page], kpe_buf.at[slot, p], sems.at[1, slot]))

    def start_seq(seq, slot):
        def body(p, c):
            for cp in page_copies(seq, slot, p):
                cp.start()
            return c
        lax.fori_loop(0, n_pages, body, 0)

    def wait_seq(seq, slot):
        def body(p, c):
            for cp in page_copies(seq, slot, p):
                cp.wait()
            return c
        lax.fori_loop(0, n_pages, body, 0)

    slot = b % 2

    @pl.when(b == 0)
    def _():
        start_seq(0, 0)

    @pl.when(b + 1 < nb)
    def _():
        start_seq(b + 1, 1 - slot)

    wait_seq(b, slot)

    page_rows = ckv_buf.shape[2]
    n_keys = n_pages * page_rows
    q = q_ref[0]
    ck = ckv_buf[slot].reshape(n_keys, kv_rank).astype(BF16)
    q_pe = q[:, kv_rank:kv_rank + MLA_ROPE]
    s_pe = [_dot(q_pe, kpe_buf[slot, p].astype(BF16)) for p in range(n_pages)]
    s = _dot_nt(q[:, 0:kv_rank], ck) + jnp.concatenate(s_pe, axis=1)
    kcn = kcn_ref[0]
    s_new = jnp.sum(q.astype(F32) * kcn.astype(F32), axis=1, keepdims=True)
    m = jnp.maximum(jnp.max(s, axis=1, keepdims=True), s_new)
    p = jnp.exp2(s - m)
    p_new = jnp.exp2(s_new - m)
    l = jnp.sum(p, axis=1, keepdims=True) + p_new
    v_new = kcn[:, 0:kv_rank].astype(F32)
    p_new_r = p_new.astype(BF16).astype(F32)
    o = _dot(p.astype(BF16), ck) + p_new_r * v_new
    o_ref[0] = (o / l).astype(o_ref.dtype)


def _mix_out_kernel(h_ref, pool_ref, olat_ref, conv_ref, wuv_ref, wo_ref, gffn_ref, wg_ref, wu_ref,
                    wd_ref, gfin_ref, out_ref, act_ref, *, final, ff_chunks):
    pairs = []
    for p in range(HEAD_PAIRS):
        o2 = jnp.concatenate([olat_ref[0, 2 * p], olat_ref[0, 2 * p + 1]], axis=-1)
        pairs.append(_dot(o2, wuv_ref[p]).astype(BF16))
    mixed = jnp.concatenate([pool_ref[0]] + pairs + [conv_ref[0]], axis=-1)
    h1 = h_ref[0] + _dot(mixed, wo_ref[...])
    hn = _rms(h1, gffn_ref[...]).astype(BF16)
    for c0, c1 in ff_chunks:
        gt = _dot(hn, wg_ref[:, c0:c1])
        up = _dot(hn, wu_ref[:, c0:c1])
        act_ref[:, c0:c1] = (gt * jax.nn.sigmoid(gt) * up).astype(BF16)
    h2 = h1 + _dot(act_ref[...], wd_ref[...])
    if final:
        h2 = _rms(h2, gfin_ref[...])
    out_ref[0] = h2


def _const_spec(shape):
    nd = len(shape)
    return pl.BlockSpec(shape, lambda *_: (0,) * nd, pipeline_mode=pl.Buffered(1))


def _params(n_axes):
    return pltpu.CompilerParams(dimension_semantics=("arbitrary",) * n_axes,
                                vmem_limit_bytes=VMEM_LIMIT_BYTES)


def _row_block(t, target):
    tm = min(t, target)
    assert t % tm == 0 and tm % 16 == 0, (t, tm)
    return tm


def _mix_in_prompt(h, cos, sin, lw, start):
    b, t, d = h.shape
    tm = _row_block(t, 512)
    pool_ch, conv_ch = lw["w_pool"].shape[0], lw["w_pw"].shape[0]
    kv_rank = lw["g_kv"].shape[1]
    qk_w = kv_rank + LANES
    weights = [lw[k] for k in ("g_mix", "w_in", "w_pool", "pool_scale", "g_q", "w_q", "w_uk",
                               "g_kv", "w_dw", "b_dw", "g_ln", "b_ln", "w_pw")]
    row = lambda w: pl.BlockSpec((1, tm, w), lambda bi, ti: (bi, ti, 0))
    in_specs = ([row(d), pl.BlockSpec((tm, cos.shape[1]), lambda bi, ti: (ti, 0)),
                 pl.BlockSpec((tm, sin.shape[1]), lambda bi, ti: (ti, 0))]
                + [_const_spec(w.shape) for w in weights])
    out_shape = (jax.ShapeDtypeStruct((b, t, pool_ch), BF16),
                 jax.ShapeDtypeStruct((b, t, conv_ch), BF16),
                 jax.ShapeDtypeStruct((b, MLA_HEADS, t, qk_w), BF16),
                 jax.ShapeDtypeStruct((b, t, qk_w), BF16),
                 jax.ShapeDtypeStruct((b, t, kv_rank), F32),
                 jax.ShapeDtypeStruct((b, t, MLA_ROPE), F32),
                 jax.ShapeDtypeStruct((b, POOL_HALO, pool_ch), F32),
                 jax.ShapeDtypeStruct((b, CONV_HALO, conv_ch), F32))
    out_specs = (row(pool_ch), row(conv_ch),
                 pl.BlockSpec((1, MLA_HEADS, tm, qk_w), lambda bi, ti: (bi, 0, ti, 0)),
                 row(qk_w), row(kv_rank), row(MLA_ROPE),
                 pl.BlockSpec((1, POOL_HALO, pool_ch), lambda bi, ti: (bi, 0, 0)),
                 pl.BlockSpec((1, CONV_HALO, conv_ch), lambda bi, ti: (bi, 0, 0)))
    return pl.pallas_call(
        functools.partial(_mix_in_kernel, decode=False, start=start, block_rows=tm),
        grid=(b, t // tm), in_specs=in_specs, out_specs=out_specs, out_shape=out_shape,
        scratch_shapes=[pltpu.VMEM((tm + POOL_HALO, pool_ch), F32),
                        pltpu.VMEM((tm + CONV_HALO, conv_ch), F32)],
        compiler_params=_params(2), name="mix_in_prompt",
    )(h, cos, sin, *weights)


def _mix_in_decode(h, cos, sin, lw, pool_state_t, conv_state_t, start):
    _, n, d = h.shape
    pool_ch, conv_ch = lw["w_pool"].shape[0], lw["w_pw"].shape[0]
    kv_rank = lw["g_kv"].shape[1]
    qk_w = kv_rank + LANES
    weights = [lw[k] for k in ("g_mix", "w_in", "w_pool", "pool_scale", "g_q", "w_q_dec", "w_uk",
                               "g_kv", "w_dw", "b_dw", "g_ln", "b_ln", "w_pw")]
    full = lambda shape: pl.BlockSpec(shape, lambda i: (0,) * len(shape))
    args = [h, cos, sin] + weights + [pool_state_t, conv_state_t]
    out_shape = (jax.ShapeDtypeStruct((1, n, pool_ch), BF16),
                 jax.ShapeDtypeStruct((1, n, conv_ch), BF16),
                 jax.ShapeDtypeStruct((1, MLA_HEADS, n, qk_w), BF16),
                 jax.ShapeDtypeStruct((1, n, qk_w), BF16),
                 jax.ShapeDtypeStruct((1, n, kv_rank), F32),
                 jax.ShapeDtypeStruct((1, n, MLA_ROPE), F32),
                 jax.ShapeDtypeStruct((n, pool_ch), F32),
                 jax.ShapeDtypeStruct((n, conv_ch), F32))
    return pl.pallas_call(
        functools.partial(_mix_in_kernel, decode=True, start=start, block_rows=n),
        grid=(1,), in_specs=[full(a.shape) for a in args],
        out_specs=tuple(full(s.shape) for s in out_shape), out_shape=out_shape,
        compiler_params=_params(1), name="mix_in_decode",
    )(*args)


def _flash_attention(q, kc, kv_rank):
    b, hds, t, qk_w = q.shape
    tq = _row_block(t, 512)
    tk = _row_block(t, 512)
    rows = hds * tq
    pairs = [(i, j) for i in range(t // tq) for j in range(((i + 1) * tq - 1) // tk + 1)]
    itab = jnp.asarray([p[0] for p in pairs], jnp.int32)
    jtab = jnp.asarray([p[1] for p in pairs], jnp.int32)
    grid_spec = pltpu.PrefetchScalarGridSpec(
        num_scalar_prefetch=2, grid=(b, len(pairs)),
        in_specs=[pl.BlockSpec((1, hds, tq, qk_w), lambda bi, s, it, jt: (bi, 0, it[s], 0)),
                  pl.BlockSpec((1, tk, qk_w), lambda bi, s, it, jt: (bi, jt[s], 0))],
        out_specs=pl.BlockSpec((1, hds, tq, kv_rank), lambda bi, s, it, jt: (bi, 0, it[s], 0)),
        scratch_shapes=[pltpu.VMEM((rows, LANES), F32), pltpu.VMEM((rows, LANES), F32),
                        pltpu.VMEM((rows, kv_rank), F32)])
    return pl.pallas_call(
        functools.partial(_flash_kernel, tq=tq, tk=tk, kv_rank=kv_rank),
        grid_spec=grid_spec, out_shape=jax.ShapeDtypeStruct((b, hds, t, kv_rank), BF16),
        compiler_params=_params(2), name="flash_attention",
    )(itab, jtab, q, kc)


def _paged_attention(page_table, q, kc_new, cache_ckv, cache_kpe_t, layer):
    n, hds, qk_w = q.shape
    n_pages = page_table.shape[1]
    page_rows, kv_rank = cache_ckv.shape[2], cache_ckv.shape[3]
    grid_spec = pltpu.PrefetchScalarGridSpec(
        num_scalar_prefetch=1, grid=(n,),
        in_specs=[pl.BlockSpec((1, hds, qk_w), lambda bi, pt: (bi, 0, 0)),
                  pl.BlockSpec((1, 1, qk_w), lambda bi, pt: (bi, 0, 0)),
                  pl.BlockSpec(memory_space=pl.ANY),
                  pl.BlockSpec(memory_space=pl.ANY)],
        out_specs=pl.BlockSpec((1, hds, kv_rank), lambda bi, pt: (bi, 0, 0)),
        scratch_shapes=[pltpu.VMEM((2, n_pages, page_rows, kv_rank), F32),
                        pltpu.VMEM((2, n_pages, MLA_ROPE, page_rows), F32),
                        pltpu.SemaphoreType.DMA((2, 2))])
    return pl.pallas_call(
        functools.partial(_paged_kernel, layer=layer, n_pages=n_pages, kv_rank=kv_rank),
        grid_spec=grid_spec, out_shape=jax.ShapeDtypeStruct((n, hds, kv_rank), BF16),
        compiler_params=_params(1), name="paged_attention",
    )(page_table.reshape(-1), q, kc_new, cache_ckv, cache_kpe_t)


def _ff_chunks(d_ff, width=512):
    return tuple((c, min(c + width, d_ff)) for c in range(0, d_ff, width))


def _mix_out(h, pool_out, o_lat, conv_out, lw, g_final, final):
    b, t, d = h.shape
    tm = _row_block(t, 512)
    pool_ch, conv_ch = pool_out.shape[2], conv_out.shape[2]
    kv_rank = o_lat.shape[3]
    d_ff = lw["w_gate"].shape[1]
    weights = [lw[k] for k in ("w_uv", "w_out", "g_ffn", "w_gate", "w_up", "w_down")] + [g_final]
    row = lambda w: pl.BlockSpec((1, tm, w), lambda bi, ti: (bi, ti, 0))
    in_specs = ([row(d), row(pool_ch),
                 pl.BlockSpec((1, MLA_HEADS, tm, kv_rank), lambda bi, ti: (bi, 0, ti, 0)),
                 row(conv_ch)] + [_const_spec(w.shape) for w in weights])
    return pl.pallas_call(
        functools.partial(_mix_out_kernel, final=final, ff_chunks=_ff_chunks(d_ff)),
        grid=(b, t // tm), in_specs=in_specs, out_specs=row(d),
        out_shape=jax.ShapeDtypeStruct((b, t, d), F32),
        scratch_shapes=[pltpu.VMEM((tm, d_ff), BF16)],
        compiler_params=_params(2), name="mix_out",
    )(h, pool_out, o_lat, conv_out, *weights)


def _rope_tables(start, t):
    inv = ROPE_THETA ** (-jnp.arange(0, MLA_ROPE, 2, dtype=F32) / MLA_ROPE)
    ang = (start + jnp.arange(t)).astype(F32)[:, None] * inv[None, :]
    cos, sin = jnp.cos(ang), jnp.sin(ang)
    reps = MLA_HEADS
    return (jnp.tile(jnp.concatenate([cos, cos], axis=1), (1, reps)),
            jnp.tile(jnp.concatenate([-sin, sin], axis=1), (1, reps)))


def _block_diag(blocks):
    n, r, c = blocks.shape[-3:]
    eye = jnp.eye(n, dtype=blocks.dtype)
    out = blocks[..., :, :, None, :] * eye[:, None, :, None]
    return out.reshape(blocks.shape[:-3] + (n * r, n * c))


def _prepare_weights(g_mix_norm, w_in, w_pool, pool_scale, g_q_norm, w_uq, g_kv_norm, w_uk, w_uv,
                     w_dw, b_dw, g_conv_ln, b_conv_ln, w_conv_pw, w_out, g_ffn_norm, w_gate, w_up,
                     w_down):
    depth = w_in.shape[0]
    pool_ch = pool_scale.shape[1]
    q_rank = g_q_norm.shape[1]
    kv_rank = g_kv_norm.shape[1]
    conv_ch = w_conv_pw.shape[1]
    s0 = pool_ch
    s1 = s0 + q_rank
    s2 = s1 + kv_rank
    s3 = s2 + MLA_ROPE
    w_in_p = jnp.concatenate(
        [w_in[:, :, 0:s0], w_in[:, :, s1:s2], w_in[:, :, s0:s1],
         jnp.tile(w_in[:, :, s2:s3], (1, 1, ROPE_GROUPS)),
         w_in[:, :, s3:s3 + conv_ch], w_in[:, :, s3 + conv_ch:s3 + 2 * conv_ch]],
        axis=2).astype(BF16)
    nope = w_uq[..., :MLA_NOPE].reshape(depth, q_rank, MLA_HEADS * MLA_NOPE)
    pe = w_uq[..., MLA_NOPE:]
    pe_dense = pe.reshape(depth, q_rank, MLA_HEADS * MLA_ROPE)
    pe_pad = jnp.pad(pe, ((0, 0), (0, 0), (0, 0), (0, LANES - MLA_ROPE)))
    pe_pad = pe_pad.reshape(depth, q_rank, MLA_HEADS * LANES)
    w_q = jnp.concatenate([nope, pe_dense], axis=2).astype(BF16)
    w_q_dec = jnp.concatenate([nope, pe_pad], axis=2).astype(BF16)
    w_uk_t = jnp.swapaxes(w_uk, 2, 3).reshape(depth, HEAD_PAIRS, 2, MLA_NOPE, kv_rank)
    w_uk_p = _block_diag(w_uk_t).astype(BF16)
    w_uv_p = _block_diag(w_uv.reshape(depth, HEAD_PAIRS, 2, kv_rank, w_uv.shape[3])).astype(BF16)
    vec = lambda a: a.reshape(depth, 1, a.shape[1])
    stacked = dict(
        g_mix=vec(g_mix_norm), w_in=w_in_p, w_pool=_block_diag(w_pool).astype(BF16),
        pool_scale=vec(pool_scale), g_q=vec(g_q_norm), w_q=w_q, w_q_dec=w_q_dec, w_uk=w_uk_p,
        g_kv=vec(g_kv_norm), w_dw=w_dw, b_dw=vec(b_dw), g_ln=vec(g_conv_ln), b_ln=vec(b_conv_ln),
        w_pw=w_conv_pw.astype(BF16), w_uv=w_uv_p, w_out=w_out.astype(BF16), g_ffn=vec(g_ffn_norm),
        w_gate=w_gate.astype(BF16), w_up=w_up.astype(BF16), w_down=w_down.astype(BF16))
    return [{k: v[l] for k, v in stacked.items()} for l in range(depth)]


def kernel(x_prompt, x_sample, cache_ckv, cache_kpe, page_table, state_pool, state_conv, g_mix_norm, w_in, w_pool, pool_scale, g_q_norm, w_uq, g_kv_norm, w_uk, w_uv, w_dw, b_dw, g_conv_ln, b_conv_ln, w_conv_pw, w_out, g_ffn_norm, w_gate, w_up, w_down, g_final):
    depth = w_in.shape[0]
    layers = _prepare_weights(g_mix_norm, w_in, w_pool, pool_scale, g_q_norm, w_uq, g_kv_norm, w_uk,
                              w_uv, w_dw, b_dw, g_conv_ln, b_conv_ln, w_conv_pw, w_out, g_ffn_norm,
                              w_gate, w_up, w_down)
    g_fin = g_final.reshape(1, -1)
    kv_rank = g_kv_norm.shape[1]

    seq = x_prompt.shape[1]
    cos_p, sin_p = _rope_tables(0, seq)
    h = x_prompt
    ckvs, kpes, pools, convs = [], [], [], []
    for l, lw in enumerate(layers):
        pool_out, conv_out, q, kc, ckv, kpe, pst, cst = _mix_in_prompt(h, cos_p, sin_p, lw, 0)
        o_lat = _flash_attention(q, kc, kv_rank)
        h = _mix_out(h, pool_out, o_lat, conv_out, lw, g_fin, l == depth - 1)
        ckvs.append(ckv)
        kpes.append(kpe)
        pools.append(pst[:, POOL_HALO - POOL_STATE:])
        convs.append(cst[:, CONV_HALO - CONV_STATE:])
    y_prompt = h
    ckv_prompt, kpe_prompt = jnp.stack(ckvs), jnp.stack(kpes)
    pool_prompt, conv_prompt = jnp.stack(pools), jnp.stack(convs)

    n_dec, dec_seq, d_model = x_sample.shape
    assert dec_seq == 1, "the sample path handles one new token per sequence"
    past_len = page_table.shape[1] * cache_ckv.shape[2]
    cos_s, sin_s = _rope_tables(past_len, 1)
    cos_s = jnp.broadcast_to(cos_s, (n_dec, cos_s.shape[1]))
    sin_s = jnp.broadcast_to(sin_s, (n_dec, sin_s.shape[1]))
    pool_t = jnp.swapaxes(state_pool, 1, 2)
    cache_kpe_t = jnp.swapaxes(cache_kpe, 2, 3)
    conv_t = jnp.swapaxes(state_conv, 1, 2)
    h = x_sample.reshape(1, n_dec, d_model)
    ckvs, kpes, pools, convs = [], [], [], []
    for l, lw in enumerate(layers):
        pool_out, conv_out, q, kc, ckv, kpe, u_new, g_new = _mix_in_decode(
            h, cos_s, sin_s, lw, pool_t[l], conv_t[l], past_len)
        o_lat = _paged_attention(page_table, jnp.swapaxes(q[0], 0, 1), kc.reshape(n_dec, 1, -1),
                                 cache_ckv, cache_kpe_t, l)
        h = _mix_out(h, pool_out, jnp.swapaxes(o_lat, 0, 1)[None], conv_out, lw, g_fin,
                     l == depth - 1)
        ckvs.append(ckv.reshape(n_dec, 1, -1))
        kpes.append(kpe.reshape(n_dec, 1, -1))
        pools.append(jnp.concatenate([state_pool[l][:, 1:], u_new[:, None]], axis=1))
        convs.append(jnp.concatenate([state_conv[l][:, 1:], g_new[:, None]], axis=1))
    y_sample = h.reshape(n_dec, 1, d_model)
    return (y_prompt, y_sample, ckv_prompt, kpe_prompt, pool_prompt, conv_prompt,
            jnp.stack(ckvs), jnp.stack(kpes), jnp.stack(pools), jnp.stack(convs))
```

```python
import functools

import jax
import jax.numpy as jnp
from jax import lax
from jax.experimental import pallas as pl
from jax.experimental.pallas import tpu as pltpu

F32 = jnp.float32
BF16 = jnp.bfloat16

POOL_WINDOWS = (2, 4, 8, 16)
POOL_STATE = max(POOL_WINDOWS) - 1
MLA_HEADS = 8
MLA_NOPE = 64
MLA_ROPE = 32
ROPE_THETA = 10000.0
SM_SCALE = (MLA_NOPE + MLA_ROPE) ** -0.5
Q_SCALE = SM_SCALE * 1.4426950408889634
DW_CONV_LEN = 31
CONV_STATE = DW_CONV_LEN - 1
NORM_EPS = 1e-6
LN_EPS = 1e-5

LANES = 128
SUBLANES = 8
VMEM_LIMIT_BYTES = 56 * 1024 * 1024

POOL_HALO = 16
CONV_HALO = 32

HEAD_PAIRS = MLA_HEADS // 2
FLASH_CHUNK_HEADS = 1
PAGED_SPLITS = 4
DMA_LOOP_UNROLL = 8
ROPE_GROUPS = LANES // MLA_ROPE


def _rms(x, g):
    return x * lax.rsqrt(jnp.mean(x * x, axis=-1, keepdims=True) + NORM_EPS) * g


def _dot(a, b):
    return jnp.dot(a, b, preferred_element_type=F32)


def _dot_nt(a, b):
    return lax.dot_general(a, b, (((1,), (1,)), ((), ())), preferred_element_type=F32)


def _rope(x, cos, sin_signed):
    width = x.shape[-1]
    half = MLA_ROPE // 2
    lane = lax.broadcasted_iota(jnp.int32, x.shape, x.ndim - 1)
    first_half = (lane % MLA_ROPE) < half
    swapped = jnp.where(first_half, pltpu.roll(x, width - half, x.ndim - 1),
                        pltpu.roll(x, half, x.ndim - 1))
    return x * cos + swapped * sin_signed


def _pool_select(s2, s4, s8, s16):
    lane = lax.broadcasted_iota(jnp.int32, s2.shape, 1)
    group = s2.shape[1] // len(POOL_WINDOWS)
    return jnp.where(lane < group, s2,
                     jnp.where(lane < 2 * group, s4, jnp.where(lane < 3 * group, s8, s16)))


def _pool_window_lanes(shape):
    lane = lax.broadcasted_iota(jnp.int32, shape, 1)
    group = shape[1] // len(POOL_WINDOWS)
    w = jnp.where(lane < group, POOL_WINDOWS[0],
                  jnp.where(lane < 2 * group, POOL_WINDOWS[1],
                            jnp.where(lane < 3 * group, POOL_WINDOWS[2], POOL_WINDOWS[3])))
    return w


def _conv_tail(y, b_dw, g_ln, b_ln, w_pw):
    y = y + b_dw
    mu = jnp.mean(y, axis=-1, keepdims=True)
    yc = y - mu
    var = jnp.mean(yc * yc, axis=-1, keepdims=True)
    yn = yc * lax.rsqrt(var + LN_EPS) * g_ln + b_ln
    act = yn * jax.nn.sigmoid(yn)
    return _dot(act.astype(BF16), w_pw)


def _mix_in_kernel(h_ref, cos_ref, sin_ref, gmix_ref, win_ref, wpool_ref, pscale_ref, gq_ref,
                   wq_ref, wuk_ref, gkv_ref, wdw_ref, bdw_ref, gln_ref, bln_ref, wpw_ref,
                   *rest, decode, start, block_rows):
    if decode:
        (pstate_ref, cstate_ref, pool_out_ref, conv_out_ref, q_ref, kc_ref, ckv_ref, kpe_ref,
         unew_ref, gnew_ref) = rest
    else:
        (pool_out_ref, conv_out_ref, q_ref, kc_ref, ckv_ref, kpe_ref, pst_ref, cst_ref,
         pext_ref, cext_ref) = rest
    tm = block_rows
    pool_ch = wpool_ref.shape[0]
    conv_ch = wpw_ref.shape[0]
    kv_rank = gkv_ref.shape[1]
    q_rank = gq_ref.shape[1]
    c_pool, c_kv = 0, pool_ch
    c_q = c_kv + kv_rank
    c_kpe = c_q + q_rank
    c_val = c_kpe + LANES
    c_gate = c_val + conv_ch

    hn = _rms(h_ref[0], gmix_ref[...]).astype(BF16)
    cos = cos_ref[...]
    sin = sin_ref[...]

    u = _dot(hn, win_ref[:, c_pool:c_pool + pool_ch])
    if decode:
        unew_ref[...] = u
        s2 = u + pstate_ref[POOL_STATE - 1]
        s4 = s2 + pstate_ref[POOL_STATE - 2] + pstate_ref[POOL_STATE - 3]
        s8 = s4
        for j in range(4, 8):
            s8 = s8 + pstate_ref[POOL_STATE - j]
        s16 = s8
        for j in range(8, 16):
            s16 = s16 + pstate_ref[POOL_STATE - j]
        wsum = _pool_select(s2, s4, s8, s16)
        cnt = jnp.minimum(start + 1, _pool_window_lanes(u.shape)).astype(F32)
    else:
        t = pl.program_id(1)

        @pl.when(t == 0)
        def _():
            pext_ref[0:POOL_HALO, :] = jnp.zeros((POOL_HALO, pool_ch), F32)
            cext_ref[0:CONV_HALO, :] = jnp.zeros((CONV_HALO, conv_ch), F32)

        pext_ref[POOL_HALO:POOL_HALO + tm, :] = u
        ext = pext_ref[...]
        s2e = ext + pltpu.roll(ext, 1, 0)
        s4e = s2e + pltpu.roll(s2e, 2, 0)
        s8e = s4e + pltpu.roll(s4e, 4, 0)
        s16e = s8e + pltpu.roll(s8e, 8, 0)
        wsum = _pool_select(s2e[POOL_HALO:], s4e[POOL_HALO:], s8e[POOL_HALO:], s16e[POOL_HALO:])
        pos = start + t * tm + lax.broadcasted_iota(jnp.int32, u.shape, 0)
        cnt = jnp.minimum(pos + 1, _pool_window_lanes(u.shape)).astype(F32)
        pext_ref[0:POOL_HALO, :] = pext_ref[tm:tm + POOL_HALO, :]
        pst_ref[0] = pext_ref[0:POOL_HALO, :]
    d = wsum / cnt - u
    pool_out_ref[0] = (_dot(d.astype(BF16), wpool_ref[...]) * pscale_ref[...]).astype(BF16)

    val = _dot(hn, win_ref[:, c_val:c_val + conv_ch])
    gate = _dot(hn, win_ref[:, c_gate:c_gate + conv_ch])
    g = val * jax.nn.sigmoid(gate)
    if decode:
        gnew_ref[...] = g
        y = g * wdw_ref[CONV_STATE:CONV_STATE + 1, :]
        for k in range(CONV_STATE):
            y = y + cstate_ref[k] * wdw_ref[k:k + 1, :]
    else:
        cext_ref[CONV_HALO:CONV_HALO + tm, :] = g
        base = CONV_HALO - CONV_STATE
        y = cext_ref[base:base + tm, :] * wdw_ref[0:1, :]
        for k in range(1, DW_CONV_LEN):
            y = y + cext_ref[base + k:base + k + tm, :] * wdw_ref[k:k + 1, :]
        cext_ref[0:CONV_HALO, :] = cext_ref[tm:tm + CONV_HALO, :]
        cst_ref[0] = cext_ref[0:CONV_HALO, :]
    conv_out_ref[0] = _conv_tail(y, bdw_ref[...], gln_ref[...], bln_ref[...],
                                 wpw_ref[...]).astype(BF16)

    ckv = _rms(_dot(hn, win_ref[:, c_kv:c_kv + kv_rank]), gkv_ref[...])
    kpe4 = _rope(_dot(hn, win_ref[:, c_kpe:c_kpe + LANES]), cos[:, :LANES], sin[:, :LANES])
    ckv_ref[0] = ckv
    kpe_ref[0] = kpe4[:, :MLA_ROPE]
    kc_ref[0, :, 0:kv_rank] = ckv.astype(BF16)
    kc_ref[0, :, kv_rank:kv_rank + LANES] = kpe4.astype(BF16)

    cq = _rms(_dot(hn, win_ref[:, c_q:c_q + q_rank]), gq_ref[...]).astype(BF16)
    nope_w = MLA_HEADS * MLA_NOPE
    q_nope = _dot(cq, wq_ref[:, 0:nope_w]).astype(BF16)
    pe_w = wq_ref.shape[1] - nope_w
    if decode:
        cos_q = jnp.concatenate([cos[:, :LANES]] * MLA_HEADS, axis=1)
        sin_q = jnp.concatenate([sin[:, :LANES]] * MLA_HEADS, axis=1)
    else:
        cos_q, sin_q = cos, sin
    q_pe = _rope(_dot(cq, wq_ref[:, nope_w:nope_w + pe_w]), cos_q, sin_q) * Q_SCALE
    lane = lax.broadcasted_iota(jnp.int32, (tm, LANES), 1)
    for p in range(HEAD_PAIRS):
        q_lat2 = _dot(q_nope[:, 2 * p * MLA_NOPE:(2 * p + 2) * MLA_NOPE], wuk_ref[p]) * Q_SCALE
        for i in range(2):
            hd = 2 * p + i
            q_ref[0, hd, :, 0:kv_rank] = q_lat2[:, i * kv_rank:(i + 1) * kv_rank].astype(BF16)
            if decode:
                pe = q_pe[:, hd * LANES:(hd + 1) * LANES]
            else:
                grp = hd // ROPE_GROUPS
                pe = jnp.where(lane // MLA_ROPE == hd % ROPE_GROUPS,
                               q_pe[:, grp * LANES:(grp + 1) * LANES], 0.0)
            q_ref[0, hd, :, kv_rank:kv_rank + LANES] = pe.astype(BF16)


def _lane_repeat(x, width):
    return jnp.concatenate([x] * (width // LANES), axis=1)


def _flash_kernel(itab_ref, jtab_ref, q_ref, kc_ref, o_ref, m_ref, l_ref, acc_ref, *, tq, tk, kv_rank):
    step = pl.program_id(1)
    i = itab_ref[step]
    j = jtab_ref[step]
    rows = MLA_HEADS * tq

    @pl.when(j == 0)
    def _():
        m_ref[...] = jnp.full(m_ref.shape, -jnp.inf, F32)
        l_ref[...] = jnp.zeros(l_ref.shape, F32)
        acc_ref[...] = jnp.zeros(acc_ref.shape, F32)

    def update(masked):
        k = kc_ref[0]
        if masked:
            chunk_rows = FLASH_CHUNK_HEADS * tq
            qpos = i * tq + lax.broadcasted_iota(
                jnp.int32, (FLASH_CHUNK_HEADS, tq, tk), 1).reshape(chunk_rows, tk)
            kpos = j * tk + lax.broadcasted_iota(jnp.int32, (chunk_rows, tk), 1)
            allowed = kpos <= qpos
        n_chunks = MLA_HEADS // FLASH_CHUNK_HEADS
        crows = FLASH_CHUNK_HEADS * tq

        def scores(c):
            hs = slice(c * FLASH_CHUNK_HEADS, (c + 1) * FLASH_CHUNK_HEADS)
            s = _dot_nt(q_ref[0, hs].reshape(crows, q_ref.shape[-1]), k)
            return jnp.where(allowed, s, -jnp.inf) if masked else s

        s_next = scores(0)
        for c in range(n_chunks):
            s = s_next
            if c + 1 < n_chunks:
                s_next = scores(c + 1)
            rs = slice(c * crows, (c + 1) * crows)
            m_prev = m_ref[rs]
            m_new = jnp.maximum(m_prev, jnp.max(s, axis=1, keepdims=True))
            alpha = jnp.exp2(m_prev - m_new)
            p = jnp.exp2(s - _lane_repeat(m_new, tk))
            l_ref[rs] = alpha * l_ref[rs] + jnp.sum(p, axis=1, keepdims=True)
            acc_ref[rs] = (_lane_repeat(alpha, kv_rank) * acc_ref[rs]
                           + _dot(p.astype(BF16), k[:, 0:kv_rank]))
            m_ref[rs] = m_new

    needs_mask = (j + 1) * tk - 1 > i * tq

    @pl.when(jnp.logical_not(needs_mask))
    def _():
        update(False)

    @pl.when(needs_mask)
    def _():
        update(True)

    @pl.when((j + 1) * tk >= (i + 1) * tq)
    def _():
        o = acc_ref[...] / _lane_repeat(l_ref[...], kv_rank)
        o_ref[0] = o.reshape(MLA_HEADS, tq, kv_rank).astype(o_ref.dtype)


def _paged_kernel(pt_ref, q_ref, kcn_ref, ckv_hbm, kpe_hbm, o_ref, ckv_buf, kpe_buf, sems,
                  *, layer, n_pages, kv_rank):
    b = pl.program_id(0)
    nb = pl.num_programs(0)

    def page_copies(seq, slot, p):
        page = pt_ref[seq * n_pages + p]
        return (pltpu.make_async_copy(ckv_hbm.at[layer, page], ckv_buf.at[slot, p], sems.at[0, slot]),
                pltpu.make_async_copy(kpe_hbm.at[layer, page], kpe_buf.at[slot, p], sems.at[1, slot]))

    def start_seq(seq, slot):
        def body(p, c):
            for cp in page_copies(seq, slot, p):
                cp.start()
            return c
        lax.fori_loop(0, n_pages, body, 0, unroll=DMA_LOOP_UNROLL)

    def wait_seq(seq, slot):
        def body(p, c):
            for cp in page_copies(seq, slot, p):
                cp.wait()
            return c
        lax.fori_loop(0, n_pages, body, 0, unroll=DMA_LOOP_UNROLL)

    slot = b % 2

    @pl.when(b == 0)
    def _():
        start_seq(0, 0)

    @pl.when(b + 1 < nb)
    def _():
        start_seq(b + 1, 1 - slot)

    wait_seq(b, slot)

    page_rows = ckv_buf.shape[2]
    q = q_ref[0]
    q_lat = q[:, 0:kv_rank]
    q_pe = q[:, kv_rank:kv_rank + MLA_ROPE]
    kcn = kcn_ref[0]
    s_new = jnp.sum(q.astype(F32) * kcn.astype(F32), axis=1, keepdims=True)
    v_new = kcn[:, 0:kv_rank].astype(F32)

    pps = n_pages // PAGED_SPLITS
    cks, scores = [], []
    for g in range(PAGED_SPLITS):
        ck = ckv_buf[slot, g * pps:(g + 1) * pps].reshape(pps * page_rows, kv_rank).astype(BF16)
        s_pe = [_dot(q_pe, kpe_buf[slot, p].astype(BF16)) for p in range(g * pps, (g + 1) * pps)]
        cks.append(ck)
        scores.append(_dot_nt(q_lat, ck) + jnp.concatenate(s_pe, axis=1))
    maxes = [jnp.max(s, axis=1, keepdims=True) for s in scores]
    parts = []
    for s, ck, mg in zip(scores, cks, maxes):
        p = jnp.exp2(s - mg)
        parts.append((mg, jnp.sum(p, axis=1, keepdims=True), _dot(p.astype(BF16), ck)))
    m = s_new
    for mg in maxes:
        m = jnp.maximum(m, mg)
    p_new = jnp.exp2(s_new - m)
    l = p_new
    o = p_new.astype(BF16).astype(F32) * v_new
    for mg, lg, og in parts:
        w = jnp.exp2(mg - m)
        l = l + w * lg
        o = o + w * og
    o_ref[0] = (o / l).astype(o_ref.dtype)


def _mix_out_kernel(h_ref, pool_ref, olat_ref, conv_ref, wuv_ref, wo_ref, gffn_ref, wg_ref, wu_ref,
                    wd_ref, gfin_ref, out_ref, act_ref, *, final, ff_chunks):
    pairs = []
    for p in range(HEAD_PAIRS):
        o2 = jnp.concatenate([olat_ref[0, 2 * p], olat_ref[0, 2 * p + 1]], axis=-1)
        pairs.append(_dot(o2, wuv_ref[p]).astype(BF16))
    mixed = jnp.concatenate([pool_ref[0]] + pairs + [conv_ref[0]], axis=-1)
    h1 = h_ref[0] + _dot(mixed, wo_ref[...])
    hn = _rms(h1, gffn_ref[...]).astype(BF16)
    for c0, c1 in ff_chunks:
        gt = _dot(hn, wg_ref[:, c0:c1])
        up = _dot(hn, wu_ref[:, c0:c1])
        act_ref[:, c0:c1] = (gt * jax.nn.sigmoid(gt) * up).astype(BF16)
    h2 = h1 + _dot(act_ref[...], wd_ref[...])
    if final:
        h2 = _rms(h2, gfin_ref[...])
    out_ref[0] = h2


def _const_spec(shape):
    nd = len(shape)
    return pl.BlockSpec(shape, lambda *_: (0,) * nd, pipeline_mode=pl.Buffered(1))


def _params(n_axes):
    return pltpu.CompilerParams(dimension_semantics=("arbitrary",) * n_axes,
                                vmem_limit_bytes=VMEM_LIMIT_BYTES)


def _row_block(t, target):
    tm = min(t, target)
    assert t % tm == 0 and tm % 16 == 0, (t, tm)
    return tm


def _mix_in_prompt(h, cos, sin, lw, start):
    b, t, d = h.shape
    tm = _row_block(t, 512)
    pool_ch, conv_ch = lw["w_pool"].shape[0], lw["w_pw"].shape[0]
    kv_rank = lw["g_kv"].shape[1]
    qk_w = kv_rank + LANES
    weights = [lw[k] for k in ("g_mix", "w_in", "w_pool", "pool_scale", "g_q", "w_q", "w_uk",
                               "g_kv", "w_dw", "b_dw", "g_ln", "b_ln", "w_pw")]
    row = lambda w: pl.BlockSpec((1, tm, w), lambda bi, ti: (bi, ti, 0))
    in_specs = ([row(d), pl.BlockSpec((tm, cos.shape[1]), lambda bi, ti: (ti, 0)),
                 pl.BlockSpec((tm, sin.shape[1]), lambda bi, ti: (ti, 0))]
                + [_const_spec(w.shape) for w in weights])
    out_shape = (jax.ShapeDtypeStruct((b, t, pool_ch), BF16),
                 jax.ShapeDtypeStruct((b, t, conv_ch), BF16),
                 jax.ShapeDtypeStruct((b, MLA_HEADS, t, qk_w), BF16),
                 jax.ShapeDtypeStruct((b, t, qk_w), BF16),
                 jax.ShapeDtypeStruct((b, t, kv_rank), F32),
                 jax.ShapeDtypeStruct((b, t, MLA_ROPE), F32),
                 jax.ShapeDtypeStruct((b, POOL_HALO, pool_ch), F32),
                 jax.ShapeDtypeStruct((b, CONV_HALO, conv_ch), F32))
    out_specs = (row(pool_ch), row(conv_ch),
                 pl.BlockSpec((1, MLA_HEADS, tm, qk_w), lambda bi, ti: (bi, 0, ti, 0)),
                 row(qk_w), row(kv_rank), row(MLA_ROPE),
                 pl.BlockSpec((1, POOL_HALO, pool_ch), lambda bi, ti: (bi, 0, 0)),
                 pl.BlockSpec((1, CONV_HALO, conv_ch), lambda bi, ti: (bi, 0, 0)))
    return pl.pallas_call(
        functools.partial(_mix_in_kernel, decode=False, start=start, block_rows=tm),
        grid=(b, t // tm), in_specs=in_specs, out_specs=out_specs, out_shape=out_shape,
        scratch_shapes=[pltpu.VMEM((tm + POOL_HALO, pool_ch), F32),
                        pltpu.VMEM((tm + CONV_HALO, conv_ch), F32)],
        compiler_params=_params(2), name="mix_in_prompt",
    )(h, cos, sin, *weights)


def _mix_in_decode(h, cos, sin, lw, pool_state_t, conv_state_t, start):
    _, n, d = h.shape
    pool_ch, conv_ch = lw["w_pool"].shape[0], lw["w_pw"].shape[0]
    kv_rank = lw["g_kv"].shape[1]
    qk_w = kv_rank + LANES
    weights = [lw[k] for k in ("g_mix", "w_in", "w_pool", "pool_scale", "g_q", "w_q_dec", "w_uk",
                               "g_kv", "w_dw", "b_dw", "g_ln", "b_ln", "w_pw")]
    full = lambda shape: pl.BlockSpec(shape, lambda i: (0,) * len(shape))
    args = [h, cos, sin] + weights + [pool_state_t, conv_state_t]
    out_shape = (jax.ShapeDtypeStruct((1, n, pool_ch), BF16),
                 jax.ShapeDtypeStruct((1, n, conv_ch), BF16),
                 jax.ShapeDtypeStruct((1, MLA_HEADS, n, qk_w), BF16),
                 jax.ShapeDtypeStruct((1, n, qk_w), BF16),
                 jax.ShapeDtypeStruct((1, n, kv_rank), F32),
                 jax.ShapeDtypeStruct((1, n, MLA_ROPE), F32),
                 jax.ShapeDtypeStruct((n, pool_ch), F32),
                 jax.ShapeDtypeStruct((n, conv_ch), F32))
    return pl.pallas_call(
        functools.partial(_mix_in_kernel, decode=True, start=start, block_rows=n),
        grid=(1,), in_specs=[full(a.shape) for a in args],
        out_specs=tuple(full(s.shape) for s in out_shape), out_shape=out_shape,
        compiler_params=_params(1), name="mix_in_decode",
    )(*args)


def _flash_attention(q, kc, kv_rank):
    b, hds, t, qk_w = q.shape
    tq = _row_block(t, 512)
    tk = _row_block(t, 512)
    rows = hds * tq
    pairs = [(i, j) for i in range(t // tq) for j in range(((i + 1) * tq - 1) // tk + 1)]
    itab = jnp.asarray([p[0] for p in pairs], jnp.int32)
    jtab = jnp.asarray([p[1] for p in pairs], jnp.int32)
    grid_spec = pltpu.PrefetchScalarGridSpec(
        num_scalar_prefetch=2, grid=(b, len(pairs)),
        in_specs=[pl.BlockSpec((1, hds, tq, qk_w), lambda bi, s, it, jt: (bi, 0, it[s], 0)),
                  pl.BlockSpec((1, tk, qk_w), lambda bi, s, it, jt: (bi, jt[s], 0))],
        out_specs=pl.BlockSpec((1, hds, tq, kv_rank), lambda bi, s, it, jt: (bi, 0, it[s], 0)),
        scratch_shapes=[pltpu.VMEM((rows, LANES), F32), pltpu.VMEM((rows, LANES), F32),
                        pltpu.VMEM((rows, kv_rank), F32)])
    return pl.pallas_call(
        functools.partial(_flash_kernel, tq=tq, tk=tk, kv_rank=kv_rank),
        grid_spec=grid_spec, out_shape=jax.ShapeDtypeStruct((b, hds, t, kv_rank), BF16),
        compiler_params=_params(2), name="flash_attention",
    )(itab, jtab, q, kc)


def _paged_attention(page_table, q, kc_new, cache_ckv, cache_kpe_t, layer):
    n, hds, qk_w = q.shape
    n_pages = page_table.shape[1]
    assert n_pages % PAGED_SPLITS == 0 and n_pages % DMA_LOOP_UNROLL == 0, n_pages
    page_rows, kv_rank = cache_ckv.shape[2], cache_ckv.shape[3]
    grid_spec = pltpu.PrefetchScalarGridSpec(
        num_scalar_prefetch=1, grid=(n,),
        in_specs=[pl.BlockSpec((1, hds, qk_w), lambda bi, pt: (bi, 0, 0)),
                  pl.BlockSpec((1, 1, qk_w), lambda bi, pt: (bi, 0, 0)),
                  pl.BlockSpec(memory_space=pl.ANY),
                  pl.BlockSpec(memory_space=pl.ANY)],
        out_specs=pl.BlockSpec((1, hds, kv_rank), lambda bi, pt: (bi, 0, 0)),
        scratch_shapes=[pltpu.VMEM((2, n_pages, page_rows, kv_rank), F32),
                        pltpu.VMEM((2, n_pages, MLA_ROPE, page_rows), F32),
                        pltpu.SemaphoreType.DMA((2, 2))])
    return pl.pallas_call(
        functools.partial(_paged_kernel, layer=layer, n_pages=n_pages, kv_rank=kv_rank),
        grid_spec=grid_spec, out_shape=jax.ShapeDtypeStruct((n, hds, kv_rank), BF16),
        compiler_params=_params(1), name="paged_attention",
    )(page_table.reshape(-1), q, kc_new, cache_ckv, cache_kpe_t)


def _ff_chunks(d_ff, width=512):
    return tuple((c, min(c + width, d_ff)) for c in range(0, d_ff, width))


def _mix_out(h, pool_out, o_lat, conv_out, lw, g_final, final):
    b, t, d = h.shape
    tm = _row_block(t, 512)
    pool_ch, conv_ch = pool_out.shape[2], conv_out.shape[2]
    kv_rank = o_lat.shape[3]
    d_ff = lw["w_gate"].shape[1]
    weights = [lw[k] for k in ("w_uv", "w_out", "g_ffn", "w_gate", "w_up", "w_down")] + [g_final]
    row = lambda w: pl.BlockSpec((1, tm, w), lambda bi, ti: (bi, ti, 0))
    in_specs = ([row(d), row(pool_ch),
                 pl.BlockSpec((1, MLA_HEADS, tm, kv_rank), lambda bi, ti: (bi, 0, ti, 0)),
                 row(conv_ch)] + [_const_spec(w.shape) for w in weights])
    return pl.pallas_call(
        functools.partial(_mix_out_kernel, final=final, ff_chunks=_ff_chunks(d_ff)),
        grid=(b, t // tm), in_specs=in_specs, out_specs=row(d),
        out_shape=jax.ShapeDtypeStruct((b, t, d), F32),
        scratch_shapes=[pltpu.VMEM((tm, d_ff), BF16)],
        compiler_params=_params(2), name="mix_out",
    )(h, pool_out, o_lat, conv_out, *weights)


def _rope_tables(start, t):
    inv = ROPE_THETA ** (-jnp.arange(0, MLA_ROPE, 2, dtype=F32) / MLA_ROPE)
    ang = (start + jnp.arange(t)).astype(F32)[:, None] * inv[None, :]
    cos, sin = jnp.cos(ang), jnp.sin(ang)
    reps = MLA_HEADS
    return (jnp.tile(jnp.concatenate([cos, cos], axis=1), (1, reps)),
            jnp.tile(jnp.concatenate([-sin, sin], axis=1), (1, reps)))


def _block_diag(blocks):
    n, r, c = blocks.shape[-3:]
    eye = jnp.eye(n, dtype=blocks.dtype)
    out = blocks[..., :, :, None, :] * eye[:, None, :, None]
    return out.reshape(blocks.shape[:-3] + (n * r, n * c))


def _prepare_weights(g_mix_norm, w_in, w_pool, pool_scale, g_q_norm, w_uq, g_kv_norm, w_uk, w_uv,
                     w_dw, b_dw, g_conv_ln, b_conv_ln, w_conv_pw, w_out, g_ffn_norm, w_gate, w_up,
                     w_down):
    depth = w_in.shape[0]
    pool_ch = pool_scale.shape[1]
    q_rank = g_q_norm.shape[1]
    kv_rank = g_kv_norm.shape[1]
    conv_ch = w_conv_pw.shape[1]
    s0 = pool_ch
    s1 = s0 + q_rank
    s2 = s1 + kv_rank
    s3 = s2 + MLA_ROPE
    w_in_p = jnp.concatenate(
        [w_in[:, :, 0:s0], w_in[:, :, s1:s2], w_in[:, :, s0:s1],
         jnp.tile(w_in[:, :, s2:s3], (1, 1, ROPE_GROUPS)),
         w_in[:, :, s3:s3 + conv_ch], w_in[:, :, s3 + conv_ch:s3 + 2 * conv_ch]],
        axis=2).astype(BF16)
    nope = w_uq[..., :MLA_NOPE].reshape(depth, q_rank, MLA_HEADS * MLA_NOPE)
    pe = w_uq[..., MLA_NOPE:]
    pe_dense = pe.reshape(depth, q_rank, MLA_HEADS * MLA_ROPE)
    pe_pad = jnp.pad(pe, ((0, 0), (0, 0), (0, 0), (0, LANES - MLA_ROPE)))
    pe_pad = pe_pad.reshape(depth, q_rank, MLA_HEADS * LANES)
    w_q = jnp.concatenate([nope, pe_dense], axis=2).astype(BF16)
    w_q_dec = jnp.concatenate([nope, pe_pad], axis=2).astype(BF16)
    w_uk_t = jnp.swapaxes(w_uk, 2, 3).reshape(depth, HEAD_PAIRS, 2, MLA_NOPE, kv_rank)
    w_uk_p = _block_diag(w_uk_t).astype(BF16)
    w_uv_p = _block_diag(w_uv.reshape(depth, HEAD_PAIRS, 2, kv_rank, w_uv.shape[3])).astype(BF16)
    vec = lambda a: a.reshape(depth, 1, a.shape[1])
    stacked = dict(
        g_mix=vec(g_mix_norm), w_in=w_in_p, w_pool=_block_diag(w_pool).astype(BF16),
        pool_scale=vec(pool_scale), g_q=vec(g_q_norm), w_q=w_q, w_q_dec=w_q_dec, w_uk=w_uk_p,
        g_kv=vec(g_kv_norm), w_dw=w_dw, b_dw=vec(b_dw), g_ln=vec(g_conv_ln), b_ln=vec(b_conv_ln),
        w_pw=w_conv_pw.astype(BF16), w_uv=w_uv_p, w_out=w_out.astype(BF16), g_ffn=vec(g_ffn_norm),
        w_gate=w_gate.astype(BF16), w_up=w_up.astype(BF16), w_down=w_down.astype(BF16))
    return [{k: v[l] for k, v in stacked.items()} for l in range(depth)]


def kernel(x_prompt, x_sample, cache_ckv, cache_kpe, page_table, state_pool, state_conv, g_mix_norm, w_in, w_pool, pool_scale, g_q_norm, w_uq, g_kv_norm, w_uk, w_uv, w_dw, b_dw, g_conv_ln, b_conv_ln, w_conv_pw, w_out, g_ffn_norm, w_gate, w_up, w_down, g_final):
    depth = w_in.shape[0]
    layers = _prepare_weights(g_mix_norm, w_in, w_pool, pool_scale, g_q_norm, w_uq, g_kv_norm, w_uk,
                              w_uv, w_dw, b_dw, g_conv_ln, b_conv_ln, w_conv_pw, w_out, g_ffn_norm,
                              w_gate, w_up, w_down)
    g_fin = g_final.reshape(1, -1)
    kv_rank = g_kv_norm.shape[1]

    seq = x_prompt.shape[1]
    cos_p, sin_p = _rope_tables(0, seq)
    h = x_prompt
    ckvs, kpes, pools, convs = [], [], [], []
    for l, lw in enumerate(layers):
        pool_out, conv_out, q, kc, ckv, kpe, pst, cst = _mix_in_prompt(h, cos_p, sin_p, lw, 0)
        o_lat = _flash_attention(q, kc, kv_rank)
        h = _mix_out(h, pool_out, o_lat, conv_out, lw, g_fin, l == depth - 1)
        ckvs.append(ckv)
        kpes.append(kpe)
        pools.append(pst[:, POOL_HALO - POOL_STATE:])
        convs.append(cst[:, CONV_HALO - CONV_STATE:])
    y_prompt = h
    ckv_prompt, kpe_prompt = jnp.stack(ckvs), jnp.stack(kpes)
    pool_prompt, conv_prompt = jnp.stack(pools), jnp.stack(convs)

    n_dec, dec_seq, d_model = x_sample.shape
    assert dec_seq == 1, "the sample path handles one new token per sequence"
    past_len = page_table.shape[1] * cache_ckv.shape[2]
    cos_s, sin_s = _rope_tables(past_len, 1)
    cos_s = jnp.broadcast_to(cos_s, (n_dec, cos_s.shape[1]))
    sin_s = jnp.broadcast_to(sin_s, (n_dec, sin_s.shape[1]))
    pool_t = jnp.swapaxes(state_pool, 1, 2)
    cache_kpe_t = jnp.swapaxes(cache_kpe, 2, 3)
    conv_t = jnp.swapaxes(state_conv, 1, 2)
    h = x_sample.reshape(1, n_dec, d_model)
    ckvs, kpes, pools, convs = [], [], [], []
    for l, lw in enumerate(layers):
        pool_out, conv_out, q, kc, ckv, kpe, u_new, g_new = _mix_in_decode(
            h, cos_s, sin_s, lw, pool_t[l], conv_t[l], past_len)
        o_lat = _paged_attention(page_table, jnp.swapaxes(q[0], 0, 1), kc.reshape(n_dec, 1, -1),
                                 cache_ckv, cache_kpe_t, l)
        h = _mix_out(h, pool_out, jnp.swapaxes(o_lat, 0, 1)[None], conv_out, lw, g_fin,
                     l == depth - 1)
        ckvs.append(ckv.reshape(n_dec, 1, -1))
        kpes.append(kpe.reshape(n_dec, 1, -1))
        pools.append(jnp.concatenate([state_pool[l][:, 1:], u_new[:, None]], axis=1))
        convs.append(jnp.concatenate([state_conv[l][:, 1:], g_new[:, None]], axis=1))
    y_sample = h.reshape(n_dec, 1, d_model)
    return (y_prompt, y_sample, ckv_prompt, kpe_prompt, pool_prompt, conv_prompt,
            jnp.stack(ckvs), jnp.stack(kpes), jnp.stack(pools), jnp.stack(convs))
```

```python
import functools

import jax
import jax.numpy as jnp
from jax import lax
from jax.experimental import pallas as pl
from jax.experimental.pallas import tpu as pltpu

F32 = jnp.float32
BF16 = jnp.bfloat16

POOL_WINDOWS = (2, 4, 8, 16)
POOL_STATE = max(POOL_WINDOWS) - 1
MLA_HEADS = 8
MLA_NOPE = 64
MLA_ROPE = 32
ROPE_THETA = 10000.0
SM_SCALE = (MLA_NOPE + MLA_ROPE) ** -0.5
Q_SCALE = SM_SCALE * 1.4426950408889634
DW_CONV_LEN = 31
CONV_STATE = DW_CONV_LEN - 1
NORM_EPS = 1e-6
LN_EPS = 1e-5

LANES = 128
SUBLANES = 8
VMEM_LIMIT_BYTES = 56 * 1024 * 1024

POOL_HALO = 16
CONV_HALO = 32

HEAD_PAIRS = MLA_HEADS // 2
FLASH_CHUNK_HEADS = 1
PAGED_SPLITS = 4
DMA_LOOP_UNROLL = 8
ROPE_GROUPS = LANES // MLA_ROPE


def _rms(x, g):
    return x * lax.rsqrt(jnp.mean(x * x, axis=-1, keepdims=True) + NORM_EPS) * g


def _dot(a, b):
    return jnp.dot(a, b, preferred_element_type=F32)


def _dot_nt(a, b):
    return lax.dot_general(a, b, (((1,), (1,)), ((), ())), preferred_element_type=F32)


def _rope(x, cos, sin_signed):
    width = x.shape[-1]
    half = MLA_ROPE // 2
    lane = lax.broadcasted_iota(jnp.int32, x.shape, x.ndim - 1)
    first_half = (lane % MLA_ROPE) < half
    swapped = jnp.where(first_half, pltpu.roll(x, width - half, x.ndim - 1),
                        pltpu.roll(x, half, x.ndim - 1))
    return x * cos + swapped * sin_signed


def _pool_select(s2, s4, s8, s16):
    lane = lax.broadcasted_iota(jnp.int32, s2.shape, 1)
    group = s2.shape[1] // len(POOL_WINDOWS)
    return jnp.where(lane < group, s2,
                     jnp.where(lane < 2 * group, s4, jnp.where(lane < 3 * group, s8, s16)))


def _pool_window_lanes(shape):
    lane = lax.broadcasted_iota(jnp.int32, shape, 1)
    group = shape[1] // len(POOL_WINDOWS)
    w = jnp.where(lane < group, POOL_WINDOWS[0],
                  jnp.where(lane < 2 * group, POOL_WINDOWS[1],
                            jnp.where(lane < 3 * group, POOL_WINDOWS[2], POOL_WINDOWS[3])))
    return w


def _conv_tail(y, b_dw, g_ln, b_ln, w_pw):
    y = y + b_dw
    mu = jnp.mean(y, axis=-1, keepdims=True)
    yc = y - mu
    var = jnp.mean(yc * yc, axis=-1, keepdims=True)
    yn = yc * lax.rsqrt(var + LN_EPS) * g_ln + b_ln
    act = yn * jax.nn.sigmoid(yn)
    return _dot(act.astype(BF16), w_pw)


def _mix_in_kernel(h_ref, cos_ref, sin_ref, gmix_ref, win_ref, wpool_ref, pscale_ref, gq_ref,
                   wq_ref, wuk_ref, gkv_ref, wdw_ref, bdw_ref, gln_ref, bln_ref, wpw_ref,
                   *rest, decode, start, block_rows):
    if decode:
        (pstate_ref, cstate_ref, pool_out_ref, conv_out_ref, q_ref, kc_ref, ckv_ref, kpe_ref,
         unew_ref, gnew_ref) = rest
    else:
        (pool_out_ref, conv_out_ref, q_ref, kc_ref, ckv_ref, kpe_ref, pst_ref, cst_ref,
         pext_ref, cext_ref, zsh_ref) = rest
    tm = block_rows
    pool_ch = wpool_ref.shape[0]
    conv_ch = wpw_ref.shape[0]
    kv_rank = gkv_ref.shape[1]
    q_rank = gq_ref.shape[1]
    c_pool, c_kv = 0, pool_ch
    c_q = c_kv + kv_rank
    c_kpe = c_q + q_rank
    c_val = c_kpe + LANES
    c_gate = c_val + conv_ch

    hn = _rms(h_ref[0], gmix_ref[...]).astype(BF16)
    cos = cos_ref[...]
    sin = sin_ref[...]

    if not decode:
        t = pl.program_id(1)

        @pl.when(t == 0)
        def _():
            pext_ref[0:POOL_HALO, :] = jnp.zeros((POOL_HALO, pool_ch), F32)
            cext_ref[0:CONV_HALO, :] = jnp.zeros((CONV_HALO, conv_ch), F32)

    cq_raw = _dot(hn, win_ref[:, c_q:c_q + q_rank])
    ckv_raw = _dot(hn, win_ref[:, c_kv:c_kv + kv_rank])
    kpe_raw = _dot(hn, win_ref[:, c_kpe:c_kpe + LANES])
    u = _dot(hn, win_ref[:, c_pool:c_pool + pool_ch])
    val = _dot(hn, win_ref[:, c_val:c_val + conv_ch])
    gate = _dot(hn, win_ref[:, c_gate:c_gate + conv_ch])

    cq = _rms(cq_raw, gq_ref[...]).astype(BF16)
    nope_w = MLA_HEADS * MLA_NOPE
    q_nope = _dot(cq, wq_ref[:, 0:nope_w]).astype(BF16)
    pe_w = wq_ref.shape[1] - nope_w
    if decode:
        cos_q = jnp.concatenate([cos[:, :LANES]] * MLA_HEADS, axis=1)
        sin_q = jnp.concatenate([sin[:, :LANES]] * MLA_HEADS, axis=1)
    else:
        cos_q, sin_q = cos, sin
    q_pe = _rope(_dot(cq, wq_ref[:, nope_w:nope_w + pe_w]), cos_q, sin_q) * Q_SCALE
    lane = lax.broadcasted_iota(jnp.int32, (tm, LANES), 1)
    for p in range(HEAD_PAIRS):
        q_lat2 = _dot(q_nope[:, 2 * p * MLA_NOPE:(2 * p + 2) * MLA_NOPE], wuk_ref[p]) * Q_SCALE
        for i in range(2):
            hd = 2 * p + i
            q_ref[0, hd, :, 0:kv_rank] = q_lat2[:, i * kv_rank:(i + 1) * kv_rank].astype(BF16)
            if decode:
                pe = q_pe[:, hd * LANES:(hd + 1) * LANES]
            else:
                grp = hd // ROPE_GROUPS
                pe = jnp.where(lane // MLA_ROPE == hd % ROPE_GROUPS,
                               q_pe[:, grp * LANES:(grp + 1) * LANES], 0.0)
            q_ref[0, hd, :, kv_rank:kv_rank + LANES] = pe.astype(BF16)


    ckv = _rms(ckv_raw, gkv_ref[...])
    kpe4 = _rope(kpe_raw, cos[:, :LANES], sin[:, :LANES])
    ckv_ref[0] = ckv
    kpe_ref[0] = kpe4[:, :MLA_ROPE]
    kc_ref[0, :, 0:kv_rank] = ckv.astype(BF16)
    kc_ref[0, :, kv_rank:kv_rank + LANES] = kpe4.astype(BF16)

    g = val * jax.nn.sigmoid(gate)
    if decode:
        gnew_ref[...] = g
        y = g * wdw_ref[CONV_STATE:CONV_STATE + 1, :]
        for k in range(CONV_STATE):
            y = y + cstate_ref[k] * wdw_ref[k:k + 1, :]
    else:
        cext_ref[CONV_HALO:CONV_HALO + tm, :] = g
        base = CONV_HALO - CONV_STATE
        y = None
        for r in range(SUBLANES):
            taps = [k for k in range(DW_CONV_LEN) if (base + k) % SUBLANES == r]
            span = tm if r == 0 else tm + SUBLANES
            z = None
            for k in taps:
                a = base + k - r
                term = cext_ref[a:a + span, :] * wdw_ref[k:k + 1, :]
                z = term if z is None else z + term
            if r:
                zsh_ref[...] = z
                z = zsh_ref[r:r + tm, :]
            y = z if y is None else y + z
        cext_ref[0:CONV_HALO, :] = cext_ref[tm:tm + CONV_HALO, :]
        cst_ref[0] = cext_ref[0:CONV_HALO, :]
    conv_out_ref[0] = _conv_tail(y, bdw_ref[...], gln_ref[...], bln_ref[...],
                                 wpw_ref[...]).astype(BF16)

    if decode:
        unew_ref[...] = u
        s2 = u + pstate_ref[POOL_STATE - 1]
        s4 = s2 + pstate_ref[POOL_STATE - 2] + pstate_ref[POOL_STATE - 3]
        s8 = s4
        for j in range(4, 8):
            s8 = s8 + pstate_ref[POOL_STATE - j]
        s16 = s8
        for j in range(8, 16):
            s16 = s16 + pstate_ref[POOL_STATE - j]
        wsum = _pool_select(s2, s4, s8, s16)
        cnt = jnp.minimum(start + 1, _pool_window_lanes(u.shape)).astype(F32)
    else:
        pext_ref[POOL_HALO:POOL_HALO + tm, :] = u
        ext = pext_ref[...]
        s2e = ext + pltpu.roll(ext, 1, 0)
        s4e = s2e + pltpu.roll(s2e, 2, 0)
        s8e = s4e + pltpu.roll(s4e, 4, 0)
        s16e = s8e + pltpu.roll(s8e, 8, 0)
        wsum = _pool_select(s2e[POOL_HALO:], s4e[POOL_HALO:], s8e[POOL_HALO:], s16e[POOL_HALO:])
        pos = start + t * tm + lax.broadcasted_iota(jnp.int32, u.shape, 0)
        cnt = jnp.minimum(pos + 1, _pool_window_lanes(u.shape)).astype(F32)
        pext_ref[0:POOL_HALO, :] = pext_ref[tm:tm + POOL_HALO, :]
        pst_ref[0] = pext_ref[0:POOL_HALO, :]
    d = wsum / cnt - u
    pool_out_ref[0] = (_dot(d.astype(BF16), wpool_ref[...]) * pscale_ref[...]).astype(BF16)


def _lane_repeat(x, width):
    return jnp.concatenate([x] * (width // LANES), axis=1)


def _flash_kernel(itab_ref, jtab_ref, q_ref, kc_ref, o_ref, m_ref, l_ref, acc_ref, *, tq, tk, kv_rank):
    step = pl.program_id(1)
    i = itab_ref[step]
    j = jtab_ref[step]
    rows = MLA_HEADS * tq

    @pl.when(j == 0)
    def _():
        m_ref[...] = jnp.full(m_ref.shape, -jnp.inf, F32)
        l_ref[...] = jnp.zeros(l_ref.shape, F32)
        acc_ref[...] = jnp.zeros(acc_ref.shape, F32)

    def update(masked):
        k = kc_ref[0]
        if masked:
            chunk_rows = FLASH_CHUNK_HEADS * tq
            qpos = i * tq + lax.broadcasted_iota(
                jnp.int32, (FLASH_CHUNK_HEADS, tq, tk), 1).reshape(chunk_rows, tk)
            kpos = j * tk + lax.broadcasted_iota(jnp.int32, (chunk_rows, tk), 1)
            allowed = kpos <= qpos
        n_chunks = MLA_HEADS // FLASH_CHUNK_HEADS
        crows = FLASH_CHUNK_HEADS * tq

        def scores(c):
            hs = slice(c * FLASH_CHUNK_HEADS, (c + 1) * FLASH_CHUNK_HEADS)
            s = _dot_nt(q_ref[0, hs].reshape(crows, q_ref.shape[-1]), k)
            return jnp.where(allowed, s, -jnp.inf) if masked else s

        s_next = scores(0)
        for c in range(n_chunks):
            s = s_next
            if c + 1 < n_chunks:
                s_next = scores(c + 1)
            rs = slice(c * crows, (c + 1) * crows)
            m_prev = m_ref[rs]
            m_new = jnp.maximum(m_prev, jnp.max(s, axis=1, keepdims=True))
            alpha = jnp.exp2(m_prev - m_new)
            p = jnp.exp2(s - _lane_repeat(m_new, tk))
            l_ref[rs] = alpha * l_ref[rs] + jnp.sum(p, axis=1, keepdims=True)
            acc_ref[rs] = (_lane_repeat(alpha, kv_rank) * acc_ref[rs]
                           + _dot(p.astype(BF16), k[:, 0:kv_rank]))
            m_ref[rs] = m_new

    needs_mask = (j + 1) * tk - 1 > i * tq

    @pl.when(jnp.logical_not(needs_mask))
    def _():
        update(False)

    @pl.when(needs_mask)
    def _():
        update(True)

    @pl.when((j + 1) * tk >= (i + 1) * tq)
    def _():
        o = acc_ref[...] / _lane_repeat(l_ref[...], kv_rank)
        o_ref[0] = o.reshape(MLA_HEADS, tq, kv_rank).astype(o_ref.dtype)


def _paged_kernel(pt_ref, q_ref, kcn_ref, ckv_hbm, kpe_hbm, o_ref, ckv_buf, kpe_buf, sems,
                  *, layer, n_pages, kv_rank):
    b = pl.program_id(0)
    nb = pl.num_programs(0)

    def page_copies(seq, slot, p):
        page = pt_ref[seq * n_pages + p]
        return (pltpu.make_async_copy(ckv_hbm.at[layer, page], ckv_buf.at[slot, p], sems.at[0, slot]),
                pltpu.make_async_copy(kpe_hbm.at[layer, page], kpe_buf.at[slot, p], sems.at[1, slot]))

    def start_seq(seq, slot):
        def body(p, c):
            for cp in page_copies(seq, slot, p):
                cp.start()
            return c
        lax.fori_loop(0, n_pages, body, 0, unroll=DMA_LOOP_UNROLL)

    def wait_seq(seq, slot):
        def body(p, c):
            for cp in page_copies(seq, slot, p):
                cp.wait()
            return c
        lax.fori_loop(0, n_pages, body, 0, unroll=DMA_LOOP_UNROLL)

    slot = b % 2

    @pl.when(b == 0)
    def _():
        start_seq(0, 0)

    @pl.when(b + 1 < nb)
    def _():
        start_seq(b + 1, 1 - slot)

    wait_seq(b, slot)

    page_rows = ckv_buf.shape[2]
    q = q_ref[0]
    q_lat = q[:, 0:kv_rank]
    q_pe = q[:, kv_rank:kv_rank + MLA_ROPE]
    kcn = kcn_ref[0]
    s_new = jnp.sum(q.astype(F32) * kcn.astype(F32), axis=1, keepdims=True)
    v_new = kcn[:, 0:kv_rank].astype(F32)

    pps = n_pages // PAGED_SPLITS
    cks, scores = [], []
    for g in range(PAGED_SPLITS):
        ck = ckv_buf[slot, g * pps:(g + 1) * pps].reshape(pps * page_rows, kv_rank).astype(BF16)
        s_pe = [_dot(q_pe, kpe_buf[slot, p].astype(BF16)) for p in range(g * pps, (g + 1) * pps)]
        cks.append(ck)
        scores.append(_dot_nt(q_lat, ck) + jnp.concatenate(s_pe, axis=1))
    maxes = [jnp.max(s, axis=1, keepdims=True) for s in scores]
    parts = []
    for s, ck, mg in zip(scores, cks, maxes):
        p = jnp.exp2(s - mg)
        parts.append((mg, jnp.sum(p, axis=1, keepdims=True), _dot(p.astype(BF16), ck)))
    m = s_new
    for mg in maxes:
        m = jnp.maximum(m, mg)
    p_new = jnp.exp2(s_new - m)
    l = p_new
    o = p_new.astype(BF16).astype(F32) * v_new
    for mg, lg, og in parts:
        w = jnp.exp2(mg - m)
        l = l + w * lg
        o = o + w * og
    o_ref[0] = (o / l).astype(o_ref.dtype)


def _mix_out_kernel(h_ref, pool_ref, olat_ref, conv_ref, wuv_ref, wo_ref, gffn_ref, wg_ref, wu_ref,
                    wd_ref, gfin_ref, out_ref, act_ref, *, final, ff_chunks):
    pairs = []
    for p in range(HEAD_PAIRS):
        o2 = jnp.concatenate([olat_ref[0, 2 * p], olat_ref[0, 2 * p + 1]], axis=-1)
        pairs.append(_dot(o2, wuv_ref[p]).astype(BF16))
    mixed = jnp.concatenate([pool_ref[0]] + pairs + [conv_ref[0]], axis=-1)
    h1 = h_ref[0] + _dot(mixed, wo_ref[...])
    hn = _rms(h1, gffn_ref[...]).astype(BF16)
    for c0, c1 in ff_chunks:
        gt = _dot(hn, wg_ref[:, c0:c1])
        up = _dot(hn, wu_ref[:, c0:c1])
        act_ref[:, c0:c1] = (gt * jax.nn.sigmoid(gt) * up).astype(BF16)
    h2 = h1 + _dot(act_ref[...], wd_ref[...])
    if final:
        h2 = _rms(h2, gfin_ref[...])
    out_ref[0] = h2


def _const_spec(shape):
    nd = len(shape)
    return pl.BlockSpec(shape, lambda *_: (0,) * nd, pipeline_mode=pl.Buffered(1))


def _params(n_axes):
    return pltpu.CompilerParams(dimension_semantics=("arbitrary",) * n_axes,
                                vmem_limit_bytes=VMEM_LIMIT_BYTES)


def _row_block(t, target):
    tm = min(t, target)
    assert t % tm == 0 and tm % 16 == 0, (t, tm)
    return tm


def _mix_in_prompt(h, cos, sin, lw, start):
    b, t, d = h.shape
    tm = _row_block(t, 512)
    pool_ch, conv_ch = lw["w_pool"].shape[0], lw["w_pw"].shape[0]
    kv_rank = lw["g_kv"].shape[1]
    qk_w = kv_rank + LANES
    weights = [lw[k] for k in ("g_mix", "w_in", "w_pool", "pool_scale", "g_q", "w_q", "w_uk",
                               "g_kv", "w_dw", "b_dw", "g_ln", "b_ln", "w_pw")]
    row = lambda w: pl.BlockSpec((1, tm, w), lambda bi, ti: (bi, ti, 0))
    in_specs = ([row(d), pl.BlockSpec((tm, cos.shape[1]), lambda bi, ti: (ti, 0)),
                 pl.BlockSpec((tm, sin.shape[1]), lambda bi, ti: (ti, 0))]
                + [_const_spec(w.shape) for w in weights])
    out_shape = (jax.ShapeDtypeStruct((b, t, pool_ch), BF16),
                 jax.ShapeDtypeStruct((b, t, conv_ch), BF16),
                 jax.ShapeDtypeStruct((b, MLA_HEADS, t, qk_w), BF16),
                 jax.ShapeDtypeStruct((b, t, qk_w), BF16),
                 jax.ShapeDtypeStruct((b, t, kv_rank), F32),
                 jax.ShapeDtypeStruct((b, t, MLA_ROPE), F32),
                 jax.ShapeDtypeStruct((b, POOL_HALO, pool_ch), F32),
                 jax.ShapeDtypeStruct((b, CONV_HALO, conv_ch), F32))
    out_specs = (row(pool_ch), row(conv_ch),
                 pl.BlockSpec((1, MLA_HEADS, tm, qk_w), lambda bi, ti: (bi, 0, ti, 0)),
                 row(qk_w), row(kv_rank), row(MLA_ROPE),
                 pl.BlockSpec((1, POOL_HALO, pool_ch), lambda bi, ti: (bi, 0, 0)),
                 pl.BlockSpec((1, CONV_HALO, conv_ch), lambda bi, ti: (bi, 0, 0)))
    return pl.pallas_call(
        functools.partial(_mix_in_kernel, decode=False, start=start, block_rows=tm),
        grid=(b, t // tm), in_specs=in_specs, out_specs=out_specs, out_shape=out_shape,
        scratch_shapes=[pltpu.VMEM((tm + POOL_HALO, pool_ch), F32),
                        pltpu.VMEM((tm + CONV_HALO, conv_ch), F32),
                        pltpu.VMEM((tm + SUBLANES, conv_ch), F32)],
        compiler_params=_params(2), name="mix_in_prompt",
    )(h, cos, sin, *weights)


def _mix_in_decode(h, cos, sin, lw, pool_state_t, conv_state_t, start):
    _, n, d = h.shape
    pool_ch, conv_ch = lw["w_pool"].shape[0], lw["w_pw"].shape[0]
    kv_rank = lw["g_kv"].shape[1]
    qk_w = kv_rank + LANES
    weights = [lw[k] for k in ("g_mix", "w_in", "w_pool", "pool_scale", "g_q", "w_q_dec", "w_uk",
                               "g_kv", "w_dw", "b_dw", "g_ln", "b_ln", "w_pw")]
    full = lambda shape: pl.BlockSpec(shape, lambda i: (0,) * len(shape))
    args = [h, cos, sin] + weights + [pool_state_t, conv_state_t]
    out_shape = (jax.ShapeDtypeStruct((1, n, pool_ch), BF16),
                 jax.ShapeDtypeStruct((1, n, conv_ch), BF16),
                 jax.ShapeDtypeStruct((1, MLA_HEADS, n, qk_w), BF16),
                 jax.ShapeDtypeStruct((1, n, qk_w), BF16),
                 jax.ShapeDtypeStruct((1, n, kv_rank), F32),
                 jax.ShapeDtypeStruct((1, n, MLA_ROPE), F32),
                 jax.ShapeDtypeStruct((n, pool_ch), F32),
                 jax.ShapeDtypeStruct((n, conv_ch), F32))
    return pl.pallas_call(
        functools.partial(_mix_in_kernel, decode=True, start=start, block_rows=n),
        grid=(1,), in_specs=[full(a.shape) for a in args],
        out_specs=tuple(full(s.shape) for s in out_shape), out_shape=out_shape,
        compiler_params=_params(1), name="mix_in_decode",
    )(*args)


def _flash_attention(q, kc, kv_rank):
    b, hds, t, qk_w = q.shape
    tq = _row_block(t, 512)
    tk = _row_block(t, 512)
    rows = hds * tq
    pairs = [(i, j) for i in range(t // tq) for j in range(((i + 1) * tq - 1) // tk + 1)]
    itab = jnp.asarray([p[0] for p in pairs], jnp.int32)
    jtab = jnp.asarray([p[1] for p in pairs], jnp.int32)
    grid_spec = pltpu.PrefetchScalarGridSpec(
        num_scalar_prefetch=2, grid=(b, len(pairs)),
        in_specs=[pl.BlockSpec((1, hds, tq, qk_w), lambda bi, s, it, jt: (bi, 0, it[s], 0)),
                  pl.BlockSpec((1, tk, qk_w), lambda bi, s, it, jt: (bi, jt[s], 0))],
        out_specs=pl.BlockSpec((1, hds, tq, kv_rank), lambda bi, s, it, jt: (bi, 0, it[s], 0)),
        scratch_shapes=[pltpu.VMEM((rows, LANES), F32), pltpu.VMEM((rows, LANES), F32),
                        pltpu.VMEM((rows, kv_rank), F32)])
    return pl.pallas_call(
        functools.partial(_flash_kernel, tq=tq, tk=tk, kv_rank=kv_rank),
        grid_spec=grid_spec, out_shape=jax.ShapeDtypeStruct((b, hds, t, kv_rank), BF16),
        compiler_params=_params(2), name="flash_attention",
    )(itab, jtab, q, kc)


def _paged_attention(page_table, q, kc_new, cache_ckv, cache_kpe_t, layer):
    n, hds, qk_w = q.shape
    n_pages = page_table.shape[1]
    assert n_pages % PAGED_SPLITS == 0 and n_pages % DMA_LOOP_UNROLL == 0, n_pages
    page_rows, kv_rank = cache_ckv.shape[2], cache_ckv.shape[3]
    grid_spec = pltpu.PrefetchScalarGridSpec(
        num_scalar_prefetch=1, grid=(n,),
        in_specs=[pl.BlockSpec((1, hds, qk_w), lambda bi, pt: (bi, 0, 0)),
                  pl.BlockSpec((1, 1, qk_w), lambda bi, pt: (bi, 0, 0)),
                  pl.BlockSpec(memory_space=pl.ANY),
                  pl.BlockSpec(memory_space=pl.ANY)],
        out_specs=pl.BlockSpec((1, hds, kv_rank), lambda bi, pt: (bi, 0, 0)),
        scratch_shapes=[pltpu.VMEM((2, n_pages, page_rows, kv_rank), F32),
                        pltpu.VMEM((2, n_pages, MLA_ROPE, page_rows), F32),
                        pltpu.SemaphoreType.DMA((2, 2))])
    return pl.pallas_call(
        functools.partial(_paged_kernel, layer=layer, n_pages=n_pages, kv_rank=kv_rank),
        grid_spec=grid_spec, out_shape=jax.ShapeDtypeStruct((n, hds, kv_rank), BF16),
        compiler_params=_params(1), name="paged_attention",
    )(page_table.reshape(-1), q, kc_new, cache_ckv, cache_kpe_t)


def _ff_chunks(d_ff, width=512):
    return tuple((c, min(c + width, d_ff)) for c in range(0, d_ff, width))


def _mix_out(h, pool_out, o_lat, conv_out, lw, g_final, final):
    b, t, d = h.shape
    tm = _row_block(t, 512)
    pool_ch, conv_ch = pool_out.shape[2], conv_out.shape[2]
    kv_rank = o_lat.shape[3]
    d_ff = lw["w_gate"].shape[1]
    weights = [lw[k] for k in ("w_uv", "w_out", "g_ffn", "w_gate", "w_up", "w_down")] + [g_final]
    row = lambda w: pl.BlockSpec((1, tm, w), lambda bi, ti: (bi, ti, 0))
    in_specs = ([row(d), row(pool_ch),
                 pl.BlockSpec((1, MLA_HEADS, tm, kv_rank), lambda bi, ti: (bi, 0, ti, 0)),
                 row(conv_ch)] + [_const_spec(w.shape) for w in weights])
    return pl.pallas_call(
        functools.partial(_mix_out_kernel, final=final, ff_chunks=_ff_chunks(d_ff)),
        grid=(b, t // tm), in_specs=in_specs, out_specs=row(d),
        out_shape=jax.ShapeDtypeStruct((b, t, d), F32),
        scratch_shapes=[pltpu.VMEM((tm, d_ff), BF16)],
        compiler_params=_params(2), name="mix_out",
    )(h, pool_out, o_lat, conv_out, *weights)


def _rope_tables(start, t):
    inv = ROPE_THETA ** (-jnp.arange(0, MLA_ROPE, 2, dtype=F32) / MLA_ROPE)
    ang = (start + jnp.arange(t)).astype(F32)[:, None] * inv[None, :]
    cos, sin = jnp.cos(ang), jnp.sin(ang)
    reps = MLA_HEADS
    return (jnp.tile(jnp.concatenate([cos, cos], axis=1), (1, reps)),
            jnp.tile(jnp.concatenate([-sin, sin], axis=1), (1, reps)))


def _block_diag(blocks):
    n, r, c = blocks.shape[-3:]
    eye = jnp.eye(n, dtype=blocks.dtype)
    out = blocks[..., :, :, None, :] * eye[:, None, :, None]
    return out.reshape(blocks.shape[:-3] + (n * r, n * c))


def _prepare_weights(g_mix_norm, w_in, w_pool, pool_scale, g_q_norm, w_uq, g_kv_norm, w_uk, w_uv,
                     w_dw, b_dw, g_conv_ln, b_conv_ln, w_conv_pw, w_out, g_ffn_norm, w_gate, w_up,
                     w_down):
    depth = w_in.shape[0]
    pool_ch = pool_scale.shape[1]
    q_rank = g_q_norm.shape[1]
    kv_rank = g_kv_norm.shape[1]
    conv_ch = w_conv_pw.shape[1]
    s0 = pool_ch
    s1 = s0 + q_rank
    s2 = s1 + kv_rank
    s3 = s2 + MLA_ROPE
    w_in_p = jnp.concatenate(
        [w_in[:, :, 0:s0], w_in[:, :, s1:s2], w_in[:, :, s0:s1],
         jnp.tile(w_in[:, :, s2:s3], (1, 1, ROPE_GROUPS)),
         w_in[:, :, s3:s3 + conv_ch], w_in[:, :, s3 + conv_ch:s3 + 2 * conv_ch]],
        axis=2).astype(BF16)
    nope = w_uq[..., :MLA_NOPE].reshape(depth, q_rank, MLA_HEADS * MLA_NOPE)
    pe = w_uq[..., MLA_NOPE:]
    pe_dense = pe.reshape(depth, q_rank, MLA_HEADS * MLA_ROPE)
    pe_pad = jnp.pad(pe, ((0, 0), (0, 0), (0, 0), (0, LANES - MLA_ROPE)))
    pe_pad = pe_pad.reshape(depth, q_rank, MLA_HEADS * LANES)
    w_q = jnp.concatenate([nope, pe_dense], axis=2).astype(BF16)
    w_q_dec = jnp.concatenate([nope, pe_pad], axis=2).astype(BF16)
    w_uk_t = jnp.swapaxes(w_uk, 2, 3).reshape(depth, HEAD_PAIRS, 2, MLA_NOPE, kv_rank)
    w_uk_p = _block_diag(w_uk_t).astype(BF16)
    w_uv_p = _block_diag(w_uv.reshape(depth, HEAD_PAIRS, 2, kv_rank, w_uv.shape[3])).astype(BF16)
    vec = lambda a: a.reshape(depth, 1, a.shape[1])
    stacked = dict(
        g_mix=vec(g_mix_norm), w_in=w_in_p, w_pool=_block_diag(w_pool).astype(BF16),
        pool_scale=vec(pool_scale), g_q=vec(g_q_norm), w_q=w_q, w_q_dec=w_q_dec, w_uk=w_uk_p,
        g_kv=vec(g_kv_norm), w_dw=w_dw, b_dw=vec(b_dw), g_ln=vec(g_conv_ln), b_ln=vec(b_conv_ln),
        w_pw=w_conv_pw.astype(BF16), w_uv=w_uv_p, w_out=w_out.astype(BF16), g_ffn=vec(g_ffn_norm),
        w_gate=w_gate.astype(BF16), w_up=w_up.astype(BF16), w_down=w_down.astype(BF16))
    return [{k: v[l] for k, v in stacked.items()} for l in range(depth)]


def kernel(x_prompt, x_sample, cache_ckv, cache_kpe, page_table, state_pool, state_conv, g_mix_norm, w_in, w_pool, pool_scale, g_q_norm, w_uq, g_kv_norm, w_uk, w_uv, w_dw, b_dw, g_conv_ln, b_conv_ln, w_conv_pw, w_out, g_ffn_norm, w_gate, w_up, w_down, g_final):
    depth = w_in.shape[0]
    layers = _prepare_weights(g_mix_norm, w_in, w_pool, pool_scale, g_q_norm, w_uq, g_kv_norm, w_uk,
                              w_uv, w_dw, b_dw, g_conv_ln, b_conv_ln, w_conv_pw, w_out, g_ffn_norm,
                              w_gate, w_up, w_down)
    g_fin = g_final.reshape(1, -1)
    kv_rank = g_kv_norm.shape[1]

    seq = x_prompt.shape[1]
    cos_p, sin_p = _rope_tables(0, seq)
    h = x_prompt
    ckvs, kpes, pools, convs = [], [], [], []
    for l, lw in enumerate(layers):
        pool_out, conv_out, q, kc, ckv, kpe, pst, cst = _mix_in_prompt(h, cos_p, sin_p, lw, 0)
        o_lat = _flash_attention(q, kc, kv_rank)
        h = _mix_out(h, pool_out, o_lat, conv_out, lw, g_fin, l == depth - 1)
        ckvs.append(ckv)
        kpes.append(kpe)
        pools.append(pst[:, POOL_HALO - POOL_STATE:])
        convs.append(cst[:, CONV_HALO - CONV_STATE:])
    y_prompt = h
    ckv_prompt, kpe_prompt = jnp.stack(ckvs), jnp.stack(kpes)
    pool_prompt, conv_prompt = jnp.stack(pools), jnp.stack(convs)

    n_dec, dec_seq, d_model = x_sample.shape
    assert dec_seq == 1, "the sample path handles one new token per sequence"
    past_len = page_table.shape[1] * cache_ckv.shape[2]
    cos_s, sin_s = _rope_tables(past_len, 1)
    cos_s = jnp.broadcast_to(cos_s, (n_dec, cos_s.shape[1]))
    sin_s = jnp.broadcast_to(sin_s, (n_dec, sin_s.shape[1]))
    pool_t = jnp.swapaxes(state_pool, 1, 2)
    cache_kpe_t = jnp.swapaxes(cache_kpe, 2, 3)
    conv_t = jnp.swapaxes(state_conv, 1, 2)
    h = x_sample.reshape(1, n_dec, d_model)
    ckvs, kpes, pools, convs = [], [], [], []
    for l, lw in enumerate(layers):
        pool_out, conv_out, q, kc, ckv, kpe, u_new, g_new = _mix_in_decode(
            h, cos_s, sin_s, lw, pool_t[l], conv_t[l], past_len)
        o_lat = _paged_attention(page_table, jnp.swapaxes(q[0], 0, 1), kc.reshape(n_dec, 1, -1),
                                 cache_ckv, cache_kpe_t, l)
        h = _mix_out(h, pool_out, jnp.swapaxes(o_lat, 0, 1)[None], conv_out, lw, g_fin,
                     l == depth - 1)
        ckvs.append(ckv.reshape(n_dec, 1, -1))
        kpes.append(kpe.reshape(n_dec, 1, -1))
        pools.append(jnp.concatenate([state_pool[l][:, 1:], u_new[:, None]], axis=1))
        convs.append(jnp.concatenate([state_conv[l][:, 1:], g_new[:, None]], axis=1))
    y_sample = h.reshape(n_dec, 1, d_model)
    return (y_prompt, y_sample, ckv_prompt, kpe_prompt, pool_prompt, conv_prompt,
            jnp.stack(ckvs), jnp.stack(kpes), jnp.stack(pools), jnp.stack(convs))
```

```python
import functools
from typing import NamedTuple

import jax
import jax.numpy as jnp
from jax import lax
from jax.experimental import pallas as pl
from jax.experimental.pallas import tpu as pltpu

F32 = jnp.float32
BF16 = jnp.bfloat16

POOL_WINDOWS = (2, 4, 8, 16)
POOL_STATE = max(POOL_WINDOWS) - 1
MLA_HEADS = 8
MLA_NOPE = 64
MLA_ROPE = 32
ROPE_THETA = 10000.0
SM_SCALE = (MLA_NOPE + MLA_ROPE) ** -0.5
Q_SCALE = SM_SCALE * 1.4426950408889634
DW_CONV_LEN = 31
CONV_STATE = DW_CONV_LEN - 1
NORM_EPS = 1e-6
LN_EPS = 1e-5

LANES = 128
SUBLANES = 8
VMEM_LIMIT_BYTES = 56 * 1024 * 1024

POOL_HALO = 16
CONV_HALO = 32

HEAD_PAIRS = MLA_HEADS // 2
FLASH_CHUNK_HEADS = 1
PAGED_SPLITS = 4
DMA_LOOP_UNROLL = 8
ROPE_GROUPS = LANES // MLA_ROPE


def _rms(x, g):
    return x * lax.rsqrt(jnp.mean(x * x, axis=-1, keepdims=True) + NORM_EPS) * g


def _dot(a, b):
    return jnp.dot(a, b, preferred_element_type=F32)


def _dot_nt(a, b):
    return lax.dot_general(a, b, (((1,), (1,)), ((), ())), preferred_element_type=F32)


def _rope(x, cos, sin_signed):
    width = x.shape[-1]
    half = MLA_ROPE // 2
    lane = lax.broadcasted_iota(jnp.int32, x.shape, x.ndim - 1)
    first_half = (lane % MLA_ROPE) < half
    swapped = jnp.where(first_half, pltpu.roll(x, width - half, x.ndim - 1),
                        pltpu.roll(x, half, x.ndim - 1))
    return x * cos + swapped * sin_signed


def _pool_select(s2, s4, s8, s16):
    lane = lax.broadcasted_iota(jnp.int32, s2.shape, 1)
    group = s2.shape[1] // len(POOL_WINDOWS)
    return jnp.where(lane < group, s2,
                     jnp.where(lane < 2 * group, s4, jnp.where(lane < 3 * group, s8, s16)))


def _pool_window_lanes(shape):
    lane = lax.broadcasted_iota(jnp.int32, shape, 1)
    group = shape[1] // len(POOL_WINDOWS)
    w = jnp.where(lane < group, POOL_WINDOWS[0],
                  jnp.where(lane < 2 * group, POOL_WINDOWS[1],
                            jnp.where(lane < 3 * group, POOL_WINDOWS[2], POOL_WINDOWS[3])))
    return w


def _conv_tail(y, b_dw, g_ln, b_ln, w_pw):
    y = y + b_dw
    mu = jnp.mean(y, axis=-1, keepdims=True)
    yc = y - mu
    var = jnp.mean(yc * yc, axis=-1, keepdims=True)
    yn = yc * lax.rsqrt(var + LN_EPS) * g_ln + b_ln
    act = yn * jax.nn.sigmoid(yn)
    return _dot(act.astype(BF16), w_pw)


def _mix_in_kernel(h_ref, cos_ref, sin_ref, gmix_ref, win_ref, wpool_ref, pscale_ref, gq_ref,
                   wq_ref, wuk_ref, gkv_ref, wdw_ref, bdw_ref, gln_ref, bln_ref, wpw_ref,
                   *rest, decode, start, block_rows):
    if decode:
        (pstate_ref, cstate_ref, pool_out_ref, conv_out_ref, q_ref, kc_ref, ckv_ref, kpe_ref,
         unew_ref, gnew_ref) = rest
    else:
        (pool_out_ref, conv_out_ref, q_ref, kc_ref, ckv_ref, kpe_ref, pst_ref, cst_ref,
         pext_ref, cext_ref, zsh_ref) = rest
    tm = block_rows
    pool_ch = wpool_ref.shape[0]
    conv_ch = wpw_ref.shape[0]
    kv_rank = gkv_ref.shape[1]
    q_rank = gq_ref.shape[1]
    c_pool, c_kv = 0, pool_ch
    c_q = c_kv + kv_rank
    c_kpe = c_q + q_rank
    c_val = c_kpe + LANES
    c_gate = c_val + conv_ch

    hn = _rms(h_ref[0], gmix_ref[...]).astype(BF16)
    cos = cos_ref[...]
    sin = sin_ref[...]

    if not decode:
        t = pl.program_id(1)

        @pl.when(t == 0)
        def _():
            pext_ref[0:POOL_HALO, :] = jnp.zeros((POOL_HALO, pool_ch), F32)
            cext_ref[0:CONV_HALO, :] = jnp.zeros((CONV_HALO, conv_ch), F32)

    cq_raw = _dot(hn, win_ref[:, c_q:c_q + q_rank])
    ckv_raw = _dot(hn, win_ref[:, c_kv:c_kv + kv_rank])
    kpe_raw = _dot(hn, win_ref[:, c_kpe:c_kpe + LANES])
    u = _dot(hn, win_ref[:, c_pool:c_pool + pool_ch])
    val = _dot(hn, win_ref[:, c_val:c_val + conv_ch])
    gate = _dot(hn, win_ref[:, c_gate:c_gate + conv_ch])

    cq = _rms(cq_raw, gq_ref[...]).astype(BF16)
    nope_w = MLA_HEADS * MLA_NOPE
    q_nope = _dot(cq, wq_ref[:, 0:nope_w]).astype(BF16)
    pe_w = wq_ref.shape[1] - nope_w
    if decode:
        cos_q = jnp.concatenate([cos[:, :LANES]] * MLA_HEADS, axis=1)
        sin_q = jnp.concatenate([sin[:, :LANES]] * MLA_HEADS, axis=1)
    else:
        cos_q, sin_q = cos, sin
    q_pe = _rope(_dot(cq, wq_ref[:, nope_w:nope_w + pe_w]), cos_q, sin_q) * Q_SCALE
    lane = lax.broadcasted_iota(jnp.int32, (tm, LANES), 1)
    for p in range(HEAD_PAIRS):
        q_lat2 = _dot(q_nope[:, 2 * p * MLA_NOPE:(2 * p + 2) * MLA_NOPE], wuk_ref[p]) * Q_SCALE
        for i in range(2):
            hd = 2 * p + i
            q_ref[0, hd, :, 0:kv_rank] = q_lat2[:, i * kv_rank:(i + 1) * kv_rank].astype(BF16)
            if decode:
                pe = q_pe[:, hd * LANES:(hd + 1) * LANES]
            else:
                grp = hd // ROPE_GROUPS
                pe = jnp.where(lane // MLA_ROPE == hd % ROPE_GROUPS,
                               q_pe[:, grp * LANES:(grp + 1) * LANES], 0.0)
            q_ref[0, hd, :, kv_rank:kv_rank + LANES] = pe.astype(BF16)


    ckv = _rms(ckv_raw, gkv_ref[...])
    kpe4 = _rope(kpe_raw, cos[:, :LANES], sin[:, :LANES])
    ckv_ref[0] = ckv
    kpe_ref[0] = kpe4[:, :MLA_ROPE]
    kc_ref[0, :, 0:kv_rank] = ckv.astype(BF16)
    kc_ref[0, :, kv_rank:kv_rank + LANES] = kpe4.astype(BF16)

    g = val * jax.nn.sigmoid(gate)
    if decode:
        gnew_ref[...] = g
        y = g * wdw_ref[CONV_STATE:CONV_STATE + 1, :]
        for k in range(CONV_STATE):
            y = y + cstate_ref[k] * wdw_ref[k:k + 1, :]
    else:
        cext_ref[CONV_HALO:CONV_HALO + tm, :] = g
        base = CONV_HALO - CONV_STATE
        y = None
        for r in range(SUBLANES):
            taps = [k for k in range(DW_CONV_LEN) if (base + k) % SUBLANES == r]
            span = tm if r == 0 else tm + SUBLANES
            z = None
            for k in taps:
                a = base + k - r
                term = cext_ref[a:a + span, :] * wdw_ref[k:k + 1, :]
                z = term if z is None else z + term
            if r:
                zsh_ref[...] = z
                z = zsh_ref[r:r + tm, :]
            y = z if y is None else y + z
        cext_ref[0:CONV_HALO, :] = cext_ref[tm:tm + CONV_HALO, :]
        cst_ref[0] = cext_ref[0:CONV_HALO, :]
    conv_out_ref[0] = _conv_tail(y, bdw_ref[...], gln_ref[...], bln_ref[...],
                                 wpw_ref[...]).astype(BF16)

    if decode:
        unew_ref[...] = u
        s2 = u + pstate_ref[POOL_STATE - 1]
        s4 = s2 + pstate_ref[POOL_STATE - 2] + pstate_ref[POOL_STATE - 3]
        s8 = s4
        for j in range(4, 8):
            s8 = s8 + pstate_ref[POOL_STATE - j]
        s16 = s8
        for j in range(8, 16):
            s16 = s16 + pstate_ref[POOL_STATE - j]
        wsum = _pool_select(s2, s4, s8, s16)
        cnt = jnp.minimum(start + 1, _pool_window_lanes(u.shape)).astype(F32)
    else:
        pext_ref[POOL_HALO:POOL_HALO + tm, :] = u
        ext = pext_ref[...]
        s2e = ext + pltpu.roll(ext, 1, 0)
        s4e = s2e + pltpu.roll(s2e, 2, 0)
        s8e = s4e + pltpu.roll(s4e, 4, 0)
        s16e = s8e + pltpu.roll(s8e, 8, 0)
        wsum = _pool_select(s2e[POOL_HALO:], s4e[POOL_HALO:], s8e[POOL_HALO:], s16e[POOL_HALO:])
        pos = start + t * tm + lax.broadcasted_iota(jnp.int32, u.shape, 0)
        cnt = jnp.minimum(pos + 1, _pool_window_lanes(u.shape)).astype(F32)
        pext_ref[0:POOL_HALO, :] = pext_ref[tm:tm + POOL_HALO, :]
        pst_ref[0] = pext_ref[0:POOL_HALO, :]
    d = wsum / cnt - u
    pool_out_ref[0] = (_dot(d.astype(BF16), wpool_ref[...]) * pscale_ref[...]).astype(BF16)


def _lane_repeat(x, width):
    return jnp.concatenate([x] * (width // LANES), axis=1)


def _flash_kernel(itab_ref, jtab_ref, q_ref, kc_ref, o_ref, m_ref, l_ref, acc_ref, *, tq, tk, kv_rank):
    step = pl.program_id(1)
    i = itab_ref[step]
    j = jtab_ref[step]
    rows = MLA_HEADS * tq

    @pl.when(j == 0)
    def _():
        m_ref[...] = jnp.full(m_ref.shape, -jnp.inf, F32)
        l_ref[...] = jnp.zeros(l_ref.shape, F32)
        acc_ref[...] = jnp.zeros(acc_ref.shape, F32)

    def update(masked):
        k = kc_ref[0]
        if masked:
            chunk_rows = FLASH_CHUNK_HEADS * tq
            qpos = i * tq + lax.broadcasted_iota(
                jnp.int32, (FLASH_CHUNK_HEADS, tq, tk), 1).reshape(chunk_rows, tk)
            kpos = j * tk + lax.broadcasted_iota(jnp.int32, (chunk_rows, tk), 1)
            allowed = kpos <= qpos
        n_chunks = MLA_HEADS // FLASH_CHUNK_HEADS
        crows = FLASH_CHUNK_HEADS * tq

        def scores(c):
            hs = slice(c * FLASH_CHUNK_HEADS, (c + 1) * FLASH_CHUNK_HEADS)
            s = _dot_nt(q_ref[0, hs].reshape(crows, q_ref.shape[-1]), k)
            return jnp.where(allowed, s, -jnp.inf) if masked else s

        s_next = scores(0)
        for c in range(n_chunks):
            s = s_next
            if c + 1 < n_chunks:
                s_next = scores(c + 1)
            rs = slice(c * crows, (c + 1) * crows)
            m_prev = m_ref[rs]
            m_new = jnp.maximum(m_prev, jnp.max(s, axis=1, keepdims=True))
            alpha = jnp.exp2(m_prev - m_new)
            p = jnp.exp2(s - _lane_repeat(m_new, tk))
            l_ref[rs] = alpha * l_ref[rs] + jnp.sum(p, axis=1, keepdims=True)
            acc_ref[rs] = (_lane_repeat(alpha, kv_rank) * acc_ref[rs]
                           + _dot(p.astype(BF16), k[:, 0:kv_rank]))
            m_ref[rs] = m_new

    needs_mask = (j + 1) * tk - 1 > i * tq

    @pl.when(jnp.logical_not(needs_mask))
    def _():
        update(False)

    @pl.when(needs_mask)
    def _():
        update(True)

    @pl.when((j + 1) * tk >= (i + 1) * tq)
    def _():
        o = acc_ref[...] / _lane_repeat(l_ref[...], kv_rank)
        o_ref[0] = o.reshape(MLA_HEADS, tq, kv_rank).astype(o_ref.dtype)


def _paged_kernel(pt_ref, q_ref, kcn_ref, ckv_hbm, kpe_hbm, o_ref, ckv_buf, kpe_buf, sems,
                  *, layer, n_pages, kv_rank):
    b = pl.program_id(0)
    nb = pl.num_programs(0)

    def page_copies(seq, slot, p):
        page = pt_ref[seq * n_pages + p]
        return (pltpu.make_async_copy(ckv_hbm.at[layer, page], ckv_buf.at[slot, p], sems.at[0, slot]),
                pltpu.make_async_copy(kpe_hbm.at[layer, page], kpe_buf.at[slot, p], sems.at[1, slot]))

    def start_seq(seq, slot):
        def body(p, c):
            for cp in page_copies(seq, slot, p):
                cp.start()
            return c
        lax.fori_loop(0, n_pages, body, 0, unroll=DMA_LOOP_UNROLL)

    def wait_seq(seq, slot):
        def body(p, c):
            for cp in page_copies(seq, slot, p):
                cp.wait()
            return c
        lax.fori_loop(0, n_pages, body, 0, unroll=DMA_LOOP_UNROLL)

    slot = b % 2

    @pl.when(b == 0)
    def _():
        start_seq(0, 0)

    @pl.when(b + 1 < nb)
    def _():
        start_seq(b + 1, 1 - slot)

    wait_seq(b, slot)

    page_rows = ckv_buf.shape[2]
    q = q_ref[0]
    q_lat = q[:, 0:kv_rank]
    q_pe = q[:, kv_rank:kv_rank + MLA_ROPE]
    kcn = kcn_ref[0]
    s_new = jnp.sum(q.astype(F32) * kcn.astype(F32), axis=1, keepdims=True)
    v_new = kcn[:, 0:kv_rank].astype(F32)

    pps = n_pages // PAGED_SPLITS
    cks, scores = [], []
    for g in range(PAGED_SPLITS):
        ck = ckv_buf[slot, g * pps:(g + 1) * pps].reshape(pps * page_rows, kv_rank).astype(BF16)
        s_pe = [_dot(q_pe, kpe_buf[slot, p].astype(BF16)) for p in range(g * pps, (g + 1) * pps)]
        cks.append(ck)
        scores.append(_dot_nt(q_lat, ck) + jnp.concatenate(s_pe, axis=1))
    maxes = [jnp.max(s, axis=1, keepdims=True) for s in scores]
    parts = []
    for s, ck, mg in zip(scores, cks, maxes):
        p = jnp.exp2(s - mg)
        parts.append((mg, jnp.sum(p, axis=1, keepdims=True), _dot(p.astype(BF16), ck)))
    m = s_new
    for mg in maxes:
        m = jnp.maximum(m, mg)
    p_new = jnp.exp2(s_new - m)
    l = p_new
    o = p_new.astype(BF16).astype(F32) * v_new
    for mg, lg, og in parts:
        w = jnp.exp2(mg - m)
        l = l + w * lg
        o = o + w * og
    o_ref[0] = (o / l).astype(o_ref.dtype)


def _mix_out_kernel(h_ref, pool_ref, olat_ref, conv_ref, wuv_ref, wo_ref, gffn_ref, wg_ref, wu_ref,
                    wd_ref, gfin_ref, out_ref, act_ref, *, final, ff_chunks):
    pairs = []
    for p in range(HEAD_PAIRS):
        o2 = jnp.concatenate([olat_ref[0, 2 * p], olat_ref[0, 2 * p + 1]], axis=-1)
        pairs.append(_dot(o2, wuv_ref[p]).astype(BF16))
    mixed = jnp.concatenate([pool_ref[0]] + pairs + [conv_ref[0]], axis=-1)
    h1 = h_ref[0] + _dot(mixed, wo_ref[...])
    hn = _rms(h1, gffn_ref[...]).astype(BF16)
    for c0, c1 in ff_chunks:
        gt = _dot(hn, wg_ref[:, c0:c1])
        up = _dot(hn, wu_ref[:, c0:c1])
        act_ref[:, c0:c1] = (gt * jax.nn.sigmoid(gt) * up).astype(BF16)
    h2 = h1 + _dot(act_ref[...], wd_ref[...])
    if final:
        h2 = _rms(h2, gfin_ref[...])
    out_ref[0] = h2


def _const_spec(shape):
    nd = len(shape)
    return pl.BlockSpec(shape, lambda *_: (0,) * nd, pipeline_mode=pl.Buffered(1))


class _LayerWeight(NamedTuple):
    stacked: jax.Array
    layer: int

    @property
    def shape(self):
        return self.stacked.shape[1:]

    def spec(self):
        index = (self.layer,) + (0,) * len(self.shape)
        return pl.BlockSpec((None,) + self.shape, lambda *_: index, pipeline_mode=pl.Buffered(1))


def _params(n_axes):
    return pltpu.CompilerParams(dimension_semantics=("arbitrary",) * n_axes,
                                vmem_limit_bytes=VMEM_LIMIT_BYTES)


def _row_block(t, target):
    tm = min(t, target)
    assert t % tm == 0 and tm % 16 == 0, (t, tm)
    return tm


def _mix_in_prompt(h, cos, sin, lw, start):
    b, t, d = h.shape
    tm = _row_block(t, 512)
    pool_ch, conv_ch = lw["w_pool"].shape[0], lw["w_pw"].shape[0]
    kv_rank = lw["g_kv"].shape[1]
    qk_w = kv_rank + LANES
    weights = [lw[k] for k in ("g_mix", "w_in", "w_pool", "pool_scale", "g_q", "w_q", "w_uk",
                               "g_kv", "w_dw", "b_dw", "g_ln", "b_ln", "w_pw")]
    row = lambda w: pl.BlockSpec((1, tm, w), lambda bi, ti: (bi, ti, 0))
    in_specs = ([row(d), pl.BlockSpec((tm, cos.shape[1]), lambda bi, ti: (ti, 0)),
                 pl.BlockSpec((tm, sin.shape[1]), lambda bi, ti: (ti, 0))]
                + [w.spec() for w in weights])
    out_shape = (jax.ShapeDtypeStruct((b, t, pool_ch), BF16),
                 jax.ShapeDtypeStruct((b, t, conv_ch), BF16),
                 jax.ShapeDtypeStruct((b, MLA_HEADS, t, qk_w), BF16),
                 jax.ShapeDtypeStruct((b, t, qk_w), BF16),
                 jax.ShapeDtypeStruct((b, t, kv_rank), F32),
                 jax.ShapeDtypeStruct((b, t, MLA_ROPE), F32),
                 jax.ShapeDtypeStruct((b, POOL_HALO, pool_ch), F32),
                 jax.ShapeDtypeStruct((b, CONV_HALO, conv_ch), F32))
    out_specs = (row(pool_ch), row(conv_ch),
                 pl.BlockSpec((1, MLA_HEADS, tm, qk_w), lambda bi, ti: (bi, 0, ti, 0)),
                 row(qk_w), row(kv_rank), row(MLA_ROPE),
                 pl.BlockSpec((1, POOL_HALO, pool_ch), lambda bi, ti: (bi, 0, 0)),
                 pl.BlockSpec((1, CONV_HALO, conv_ch), lambda bi, ti: (bi, 0, 0)))
    return pl.pallas_call(
        functools.partial(_mix_in_kernel, decode=False, start=start, block_rows=tm),
        grid=(b, t // tm), in_specs=in_specs, out_specs=out_specs, out_shape=out_shape,
        scratch_shapes=[pltpu.VMEM((tm + POOL_HALO, pool_ch), F32),
                        pltpu.VMEM((tm + CONV_HALO, conv_ch), F32),
                        pltpu.VMEM((tm + SUBLANES, conv_ch), F32)],
        compiler_params=_params(2), name="mix_in_prompt",
    )(h, cos, sin, *[w.stacked for w in weights])


def _mix_in_decode(h, cos, sin, lw, pool_state_t, conv_state_t, start):
    _, n, d = h.shape
    pool_ch, conv_ch = lw["w_pool"].shape[0], lw["w_pw"].shape[0]
    kv_rank = lw["g_kv"].shape[1]
    qk_w = kv_rank + LANES
    weights = [lw[k] for k in ("g_mix", "w_in", "w_pool", "pool_scale", "g_q", "w_q_dec", "w_uk",
                               "g_kv", "w_dw", "b_dw", "g_ln", "b_ln", "w_pw")]
    full = lambda shape: pl.BlockSpec(shape, lambda i: (0,) * len(shape))
    args = [h, cos, sin] + [w.stacked for w in weights] + [pool_state_t, conv_state_t]
    in_specs = ([full(a.shape) for a in (h, cos, sin)] + [w.spec() for w in weights]
                + [full(pool_state_t.shape), full(conv_state_t.shape)])
    out_shape = (jax.ShapeDtypeStruct((1, n, pool_ch), BF16),
                 jax.ShapeDtypeStruct((1, n, conv_ch), BF16),
                 jax.ShapeDtypeStruct((1, MLA_HEADS, n, qk_w), BF16),
                 jax.ShapeDtypeStruct((1, n, qk_w), BF16),
                 jax.ShapeDtypeStruct((1, n, kv_rank), F32),
                 jax.ShapeDtypeStruct((1, n, MLA_ROPE), F32),
                 jax.ShapeDtypeStruct((n, pool_ch), F32),
                 jax.ShapeDtypeStruct((n, conv_ch), F32))
    return pl.pallas_call(
        functools.partial(_mix_in_kernel, decode=True, start=start, block_rows=n),
        grid=(1,), in_specs=in_specs,
        out_specs=tuple(full(s.shape) for s in out_shape), out_shape=out_shape,
        compiler_params=_params(1), name="mix_in_decode",
    )(*args)


def _flash_attention(q, kc, kv_rank):
    b, hds, t, qk_w = q.shape
    tq = _row_block(t, 512)
    tk = _row_block(t, 512)
    rows = hds * tq
    pairs = [(i, j) for i in range(t // tq) for j in range(((i + 1) * tq - 1) // tk + 1)]
    itab = jnp.asarray([p[0] for p in pairs], jnp.int32)
    jtab = jnp.asarray([p[1] for p in pairs], jnp.int32)
    grid_spec = pltpu.PrefetchScalarGridSpec(
        num_scalar_prefetch=2, grid=(b, len(pairs)),
        in_specs=[pl.BlockSpec((1, hds, tq, qk_w), lambda bi, s, it, jt: (bi, 0, it[s], 0)),
                  pl.BlockSpec((1, tk, qk_w), lambda bi, s, it, jt: (bi, jt[s], 0))],
        out_specs=pl.BlockSpec((1, hds, tq, kv_rank), lambda bi, s, it, jt: (bi, 0, it[s], 0)),
        scratch_shapes=[pltpu.VMEM((rows, LANES), F32), pltpu.VMEM((rows, LANES), F32),
                        pltpu.VMEM((rows, kv_rank), F32)])
    return pl.pallas_call(
        functools.partial(_flash_kernel, tq=tq, tk=tk, kv_rank=kv_rank),
        grid_spec=grid_spec, out_shape=jax.ShapeDtypeStruct((b, hds, t, kv_rank), BF16),
        compiler_params=_params(2), name="flash_attention",
    )(itab, jtab, q, kc)


def _paged_attention(page_table, q, kc_new, cache_ckv, cache_kpe_t, layer):
    n, hds, qk_w = q.shape
    n_pages = page_table.shape[1]
    assert n_pages % PAGED_SPLITS == 0 and n_pages % DMA_LOOP_UNROLL == 0, n_pages
    page_rows, kv_rank = cache_ckv.shape[2], cache_ckv.shape[3]
    grid_spec = pltpu.PrefetchScalarGridSpec(
        num_scalar_prefetch=1, grid=(n,),
        in_specs=[pl.BlockSpec((1, hds, qk_w), lambda bi, pt: (bi, 0, 0)),
                  pl.BlockSpec((1, 1, qk_w), lambda bi, pt: (bi, 0, 0)),
                  pl.BlockSpec(memory_space=pl.ANY),
                  pl.BlockSpec(memory_space=pl.ANY)],
        out_specs=pl.BlockSpec((1, hds, kv_rank), lambda bi, pt: (bi, 0, 0)),
        scratch_shapes=[pltpu.VMEM((2, n_pages, page_rows, kv_rank), F32),
                        pltpu.VMEM((2, n_pages, MLA_ROPE, page_rows), F32),
                        pltpu.SemaphoreType.DMA((2, 2))])
    return pl.pallas_call(
        functools.partial(_paged_kernel, layer=layer, n_pages=n_pages, kv_rank=kv_rank),
        grid_spec=grid_spec, out_shape=jax.ShapeDtypeStruct((n, hds, kv_rank), BF16),
        compiler_params=_params(1), name="paged_attention",
    )(page_table.reshape(-1), q, kc_new, cache_ckv, cache_kpe_t)


def _ff_chunks(d_ff, width=512):
    return tuple((c, min(c + width, d_ff)) for c in range(0, d_ff, width))


def _mix_out(h, pool_out, o_lat, conv_out, lw, g_final, final):
    b, t, d = h.shape
    tm = _row_block(t, 512)
    pool_ch, conv_ch = pool_out.shape[2], conv_out.shape[2]
    kv_rank = o_lat.shape[3]
    d_ff = lw["w_gate"].shape[1]
    weights = [lw[k] for k in ("w_uv", "w_out", "g_ffn", "w_gate", "w_up", "w_down")]
    row = lambda w: pl.BlockSpec((1, tm, w), lambda bi, ti: (bi, ti, 0))
    in_specs = ([row(d), row(pool_ch),
                 pl.BlockSpec((1, MLA_HEADS, tm, kv_rank), lambda bi, ti: (bi, 0, ti, 0)),
                 row(conv_ch)] + [w.spec() for w in weights] + [_const_spec(g_final.shape)])
    return pl.pallas_call(
        functools.partial(_mix_out_kernel, final=final, ff_chunks=_ff_chunks(d_ff)),
        grid=(b, t // tm), in_specs=in_specs, out_specs=row(d),
        out_shape=jax.ShapeDtypeStruct((b, t, d), F32),
        scratch_shapes=[pltpu.VMEM((tm, d_ff), BF16)],
        compiler_params=_params(2), name="mix_out",
    )(h, pool_out, o_lat, conv_out, *[w.stacked for w in weights], g_final)


def _rope_tables(start, t):
    inv = ROPE_THETA ** (-jnp.arange(0, MLA_ROPE, 2, dtype=F32) / MLA_ROPE)
    ang = (start + jnp.arange(t)).astype(F32)[:, None] * inv[None, :]
    cos, sin = jnp.cos(ang), jnp.sin(ang)
    reps = MLA_HEADS
    return (jnp.tile(jnp.concatenate([cos, cos], axis=1), (1, reps)),
            jnp.tile(jnp.concatenate([-sin, sin], axis=1), (1, reps)))


def _block_diag(blocks):
    n, r, c = blocks.shape[-3:]
    eye = jnp.eye(n, dtype=blocks.dtype)
    out = blocks[..., :, :, None, :] * eye[:, None, :, None]
    return out.reshape(blocks.shape[:-3] + (n * r, n * c))


def _prepare_weights(g_mix_norm, w_in, w_pool, pool_scale, g_q_norm, w_uq, g_kv_norm, w_uk, w_uv,
                     w_dw, b_dw, g_conv_ln, b_conv_ln, w_conv_pw, w_out, g_ffn_norm, w_gate, w_up,
                     w_down):
    depth = w_in.shape[0]
    pool_ch = pool_scale.shape[1]
    q_rank = g_q_norm.shape[1]
    kv_rank = g_kv_norm.shape[1]
    conv_ch = w_conv_pw.shape[1]
    s0 = pool_ch
    s1 = s0 + q_rank
    s2 = s1 + kv_rank
    s3 = s2 + MLA_ROPE
    w_in_p = jnp.concatenate(
        [w_in[:, :, 0:s0], w_in[:, :, s1:s2], w_in[:, :, s0:s1],
         jnp.tile(w_in[:, :, s2:s3], (1, 1, ROPE_GROUPS)),
         w_in[:, :, s3:s3 + conv_ch], w_in[:, :, s3 + conv_ch:s3 + 2 * conv_ch]],
        axis=2).astype(BF16)
    nope = w_uq[..., :MLA_NOPE].reshape(depth, q_rank, MLA_HEADS * MLA_NOPE)
    pe = w_uq[..., MLA_NOPE:]
    pe_dense = pe.reshape(depth, q_rank, MLA_HEADS * MLA_ROPE)
    pe_pad = jnp.pad(pe, ((0, 0), (0, 0), (0, 0), (0, LANES - MLA_ROPE)))
    pe_pad = pe_pad.reshape(depth, q_rank, MLA_HEADS * LANES)
    w_q = jnp.concatenate([nope, pe_dense], axis=2).astype(BF16)
    w_q_dec = jnp.concatenate([nope, pe_pad], axis=2).astype(BF16)
    w_uk_t = jnp.swapaxes(w_uk, 2, 3).reshape(depth, HEAD_PAIRS, 2, MLA_NOPE, kv_rank)
    w_uk_p = _block_diag(w_uk_t).astype(BF16)
    w_uv_p = _block_diag(w_uv.reshape(depth, HEAD_PAIRS, 2, kv_rank, w_uv.shape[3])).astype(BF16)
    vec = lambda a: a.reshape(depth, 1, a.shape[1])
    stacked = dict(
        g_mix=vec(g_mix_norm), w_in=w_in_p, w_pool=_block_diag(w_pool).astype(BF16),
        pool_scale=vec(pool_scale), g_q=vec(g_q_norm), w_q=w_q, w_q_dec=w_q_dec, w_uk=w_uk_p,
        g_kv=vec(g_kv_norm), w_dw=w_dw, b_dw=vec(b_dw), g_ln=vec(g_conv_ln), b_ln=vec(b_conv_ln),
        w_pw=w_conv_pw.astype(BF16), w_uv=w_uv_p, w_out=w_out.astype(BF16), g_ffn=vec(g_ffn_norm),
        w_gate=w_gate.astype(BF16), w_up=w_up.astype(BF16), w_down=w_down.astype(BF16))
    return [{k: _LayerWeight(v, l) for k, v in stacked.items()} for l in range(depth)]


def kernel(x_prompt, x_sample, cache_ckv, cache_kpe, page_table, state_pool, state_conv, g_mix_norm, w_in, w_pool, pool_scale, g_q_norm, w_uq, g_kv_norm, w_uk, w_uv, w_dw, b_dw, g_conv_ln, b_conv_ln, w_conv_pw, w_out, g_ffn_norm, w_gate, w_up, w_down, g_final):
    depth = w_in.shape[0]
    layers = _prepare_weights(g_mix_norm, w_in, w_pool, pool_scale, g_q_norm, w_uq, g_kv_norm, w_uk,
                              w_uv, w_dw, b_dw, g_conv_ln, b_conv_ln, w_conv_pw, w_out, g_ffn_norm,
                              w_gate, w_up, w_down)
    g_fin = g_final.reshape(1, -1)
    kv_rank = g_kv_norm.shape[1]

    seq = x_prompt.shape[1]
    cos_p, sin_p = _rope_tables(0, seq)
    h = x_prompt
    ckvs, kpes, pools, convs = [], [], [], []
    for l, lw in enumerate(layers):
        pool_out, conv_out, q, kc, ckv, kpe, pst, cst = _mix_in_prompt(h, cos_p, sin_p, lw, 0)
        o_lat = _flash_attention(q, kc, kv_rank)
        h = _mix_out(h, pool_out, o_lat, conv_out, lw, g_fin, l == depth - 1)
        ckvs.append(ckv)
        kpes.append(kpe)
        pools.append(pst[:, POOL_HALO - POOL_STATE:])
        convs.append(cst[:, CONV_HALO - CONV_STATE:])
    y_prompt = h
    ckv_prompt, kpe_prompt = jnp.stack(ckvs), jnp.stack(kpes)
    pool_prompt, conv_prompt = jnp.stack(pools), jnp.stack(convs)

    n_dec, dec_seq, d_model = x_sample.shape
    assert dec_seq == 1, "the sample path handles one new token per sequence"
    past_len = page_table.shape[1] * cache_ckv.shape[2]
    cos_s, sin_s = _rope_tables(past_len, 1)
    cos_s = jnp.broadcast_to(cos_s, (n_dec, cos_s.shape[1]))
    sin_s = jnp.broadcast_to(sin_s, (n_dec, sin_s.shape[1]))
    pool_t = jnp.swapaxes(state_pool, 1, 2)
    cache_kpe_t = jnp.swapaxes(cache_kpe, 2, 3)
    conv_t = jnp.swapaxes(state_conv, 1, 2)
    h = x_sample.reshape(1, n_dec, d_model)
    ckvs, kpes, pools, convs = [], [], [], []
    for l, lw in enumerate(layers):
        pool_out, conv_out, q, kc, ckv, kpe, u_new, g_new = _mix_in_decode(
            h, cos_s, sin_s, lw, pool_t[l], conv_t[l], past_len)
        o_lat = _paged_attention(page_table, jnp.swapaxes(q[0], 0, 1), kc.reshape(n_dec, 1, -1),
                                 cache_ckv, cache_kpe_t, l)
        h = _mix_out(h, pool_out, jnp.swapaxes(o_lat, 0, 1)[None], conv_out, lw, g_fin,
                     l == depth - 1)
        ckvs.append(ckv.reshape(n_dec, 1, -1))
        kpes.append(kpe.reshape(n_dec, 1, -1))
        pools.append(jnp.concatenate([state_pool[l][:, 1:], u_new[:, None]], axis=1))
        convs.append(jnp.concatenate([state_conv[l][:, 1:], g_new[:, None]], axis=1))
    y_sample = h.reshape(n_dec, 1, d_model)
    return (y_prompt, y_sample, ckv_prompt, kpe_prompt, pool_prompt, conv_prompt,
            jnp.stack(ckvs), jnp.stack(kpes), jnp.stack(pools), jnp.stack(convs))
```
